```python
import functools
import jax, jax.numpy as jnp
from jax import lax
import numpy as np

D_MODEL = 1024
BATCH = 8
SEQ = 2048
DEPTH = 2
DEC_BATCH = 32
DEC_SEQ = 8
PAST_LEN = 16384
PAGE_SIZE = 128

N_HEADS = 8
HEAD_DIM = 64
D_ATTN = N_HEADS * HEAD_DIM
D_CONV = D_MODEL // 2
CONV_K = 31
MOBA_BLOCK = 256
MOBA_TOPK = 3
QBLK = 64
ROPE_THETA = 10000.0
D_FF = -(-8 * D_MODEL // (3 * 256)) * 256
EPS = 1e-6
SPLIT_AT = (D_ATTN, 2 * D_ATTN, 3 * D_ATTN, 3 * D_ATTN + 2 * D_CONV, 3 * D_ATTN + 2 * D_CONV + D_MODEL)
D_IN = 3 * D_ATTN + 2 * D_CONV + 2 * D_MODEL

kernel_name = 'moba_conformer_gated_hybrid_step'


def rms_norm(x, g):
    xf = x.astype(jnp.float32)
    y = xf * lax.rsqrt(jnp.mean(xf * xf, axis=-1, keepdims=True) + EPS)
    return (y * g.astype(jnp.float32)).astype(x.dtype)


def layer_norm(x, g, b):
    xf = x.astype(jnp.float32)
    mu = jnp.mean(xf, axis=-1, keepdims=True)
    d = xf - mu
    y = d * lax.rsqrt(jnp.mean(d * d, axis=-1, keepdims=True) + EPS)
    return (y * g.astype(jnp.float32) + b.astype(jnp.float32)).astype(x.dtype)


def rope(x, pos):
    half = HEAD_DIM // 2
    inv_freq = ROPE_THETA ** (-jnp.arange(half, dtype=jnp.float32) / half)
    ang = pos.astype(jnp.float32)[:, None] * inv_freq[None, :]
    cos = jnp.cos(ang)[None, :, None, :]
    sin = jnp.sin(ang)[None, :, None, :]
    xf = x.astype(jnp.float32)
    x1, x2 = xf[..., :half], xf[..., half:]
    return jnp.concatenate([x1 * cos - x2 * sin, x2 * cos + x1 * sin], axis=-1).astype(x.dtype)


def to_blocks(k_all, v_all):
    bsz, t = k_all.shape[0], k_all.shape[1]
    nb = -(-t // MOBA_BLOCK)
    pad = ((0, 0), (0, nb * MOBA_BLOCK - t), (0, 0), (0, 0))
    kb = jnp.pad(k_all, pad).reshape(bsz, nb, MOBA_BLOCK, N_HEADS, HEAD_DIM)
    vb = jnp.pad(v_all, pad).reshape(bsz, nb, MOBA_BLOCK, N_HEADS, HEAD_DIM)
    kmean = jnp.mean(kb, axis=2, dtype=jnp.float32)
    return kb, vb, kmean, nb


def moba_attend(q, q_pos, kb, vb, kmean, n_sel):
    bsz, nq, nh = q.shape[0], q.shape[1], q.shape[2]
    nb = kb.shape[1]
    qb = q_pos // MOBA_BLOCK
    gate = jnp.einsum('bqhd,bnhd->bhqn', q.astype(jnp.float32), kmean)
    fully_past = jnp.arange(nb)[None, :] < qb[:, None]
    gate = jnp.where(fully_past[None, None], gate, -jnp.inf)
    _, sel = lax.top_k(gate, n_sel)
    sel_ok = sel < qb[None, None, :, None]
    own = jnp.broadcast_to(qb[None, None, :, None], (bsz, nh, nq, 1))
    blk = jnp.concatenate([sel, own], axis=-1)
    ok = jnp.concatenate([sel_ok, jnp.ones(own.shape, dtype=bool)], axis=-1)
    bi = jnp.arange(bsz)[:, None, None, None]
    hi = jnp.arange(nh)[None, :, None, None]
    k_g = kb[bi, blk, :, hi]
    v_g = vb[bi, blk, :, hi]
    kpos = blk[..., None] * MOBA_BLOCK + jnp.arange(MOBA_BLOCK)
    mask = ok[..., None] & (kpos <= q_pos[None, None, :, None, None])
    logits = jnp.einsum('bqhd,bhqnkd->bhqnk', q, k_g).astype(jnp.float32) * (HEAD_DIM ** -0.5)
    logits = jnp.where(mask, logits, -jnp.inf)
    p = jax.nn.softmax(logits.reshape(bsz, nh, nq, -1), axis=-1).reshape(logits.shape)
    return jnp.einsum('bhqnk,bhqnkd->bqhd', p.astype(v_g.dtype), v_g)


def moba_prompt(q, k, v):
    bsz, s = q.shape[0], q.shape[1]
    kb, vb, kmean, nb = to_blocks(k, v)
    n_chunks = s // QBLK
    qc = q.reshape(bsz, n_chunks, QBLK, N_HEADS, HEAD_DIM).transpose(1, 0, 2, 3, 4)
    pos = jnp.arange(s, dtype=jnp.int32).reshape(n_chunks, QBLK)
    n_sel = min(MOBA_TOPK, nb)
    out = lax.map(lambda a: moba_attend(a[0], a[1], kb, vb, kmean, n_sel), (qc, pos))
    return out.transpose(1, 0, 2, 3, 4).reshape(bsz, s, N_HEADS, HEAD_DIM)


def moba_sample(cache_k, cache_v, layer, page_table, q, k, v):
    db, t_new = q.shape[0], q.shape[1]
    past = page_table.shape[1] * PAGE_SIZE
    k_past = cache_k[layer, page_table].reshape(db, past, N_HEADS, HEAD_DIM)
    v_past = cache_v[layer, page_table].reshape(db, past, N_HEADS, HEAD_DIM)
    k_all = jnp.concatenate([k_past, k], axis=1)
    v_all = jnp.concatenate([v_past, v], axis=1)
    kb, vb, kmean, nb = to_blocks(k_all, v_all)
    q_pos = past + jnp.arange(t_new, dtype=jnp.int32)
    return moba_attend(q, q_pos, kb, vb, kmean, min(MOBA_TOPK, nb))


def depthwise_causal_conv(u_ctx, w, b):
    out = lax.conv_general_dilated(u_ctx, w[:, None, :], window_strides=(1,), padding='VALID',
                                   dimension_numbers=('NWC', 'WIO', 'NWC'),
                                   feature_group_count=u_ctx.shape[-1])
    return out + b


def layer_forward(x, c, pos, conv_left, attend, w_ada, b_ada, g_mix, w_in, w_attn_o,
                  conv_w, conv_b, conv_norm_g, conv_norm_b, w_conv_o, w_out,
                  g_ffn, w_ffn_in, w_ffn_o):
    bsz, t = x.shape[0], x.shape[1]
    mod = (jax.nn.silu(c) @ w_ada + b_ada)[:, None, :]
    sh_m, sc_m, gt_m, sh_f, sc_f, gt_f = jnp.split(mod, 6, axis=-1)
    h = rms_norm(x, g_mix) * (1 + sc_m) + sh_m
    q, k, v, u, g_a, g_b = jnp.split(h @ w_in, SPLIT_AT, axis=-1)
    q = rope(q.reshape(bsz, t, N_HEADS, HEAD_DIM), pos)
    k = rope(k.reshape(bsz, t, N_HEADS, HEAD_DIM), pos)
    v = v.reshape(bsz, t, N_HEADS, HEAD_DIM)
    a = attend(q, k, v).reshape(bsz, t, D_ATTN) @ w_attn_o
    u = u[..., :D_CONV] * jax.nn.sigmoid(u[..., D_CONV:])
    u_ctx = jnp.concatenate([conv_left, u], axis=1)
    conv_state = u_ctx[:, -(CONV_K - 1):]
    cb = depthwise_causal_conv(u_ctx, conv_w, conv_b)
    cb = jax.nn.silu(layer_norm(cb, conv_norm_g, conv_norm_b)) @ w_conv_o
    merged = jax.nn.sigmoid(g_a) * a + jax.nn.sigmoid(g_b) * cb
    x = x + gt_m * (merged @ w_out)
    h = rms_norm(x, g_ffn) * (1 + sc_f) + sh_f
    f_gate, f_up = jnp.split(h @ w_ffn_in, 2, axis=-1)
    x = x + gt_f * ((jax.nn.silu(f_gate) * f_up) @ w_ffn_o)
    return x, k, v, conv_state


def setup_inputs(seed: int = 0) -> dict:
    key = jax.random.key(seed)
    ks = jax.random.split(key, 24)
    n_pages = PAST_LEN // PAGE_SIZE
    n_phys = (DEC_BATCH * n_pages * 5) // 4

    def nrm(k, shape, scale):
        return jax.random.normal(k, shape, jnp.float32) * scale

    page_table = jax.random.permutation(ks[7], n_phys)[:DEC_BATCH * n_pages]
    page_table = page_table.reshape(DEC_BATCH, n_pages).astype(jnp.int32)
    return {
        'x_prompt': nrm(ks[0], (BATCH, SEQ, D_MODEL), 1.0),
        'x_sample': nrm(ks[1], (DEC_BATCH, DEC_SEQ, D_MODEL), 1.0),
        'cache_k': nrm(ks[2], (DEPTH, n_phys, PAGE_SIZE, N_HEADS, HEAD_DIM), 1.0),
        'cache_v': nrm(ks[3], (DEPTH, n_phys, PAGE_SIZE, N_HEADS, HEAD_DIM), 1.0),
        'state_conv': nrm(ks[4], (DEPTH, DEC_BATCH, CONV_K - 1, D_CONV), 0.5),
        'page_table': page_table,
        'c_prompt': nrm(ks[5], (BATCH, D_MODEL), 1.0),
        'c_sample': nrm(ks[6], (DEC_BATCH, D_MODEL), 1.0),
        'w_ada': nrm(ks[8], (DEPTH, D_MODEL, 6 * D_MODEL), 0.5 * D_MODEL ** -0.5),
        'b_ada': nrm(ks[9], (DEPTH, 6 * D_MODEL), 0.02),
        'g_mix': 1.0 + nrm(ks[10], (DEPTH, D_MODEL), 0.1),
        'w_in': nrm(ks[11], (DEPTH, D_MODEL, D_IN), D_MODEL ** -0.5),
        'w_attn_o': nrm(ks[12], (DEPTH, D_ATTN, D_MODEL), D_ATTN ** -0.5),
        'conv_w': nrm(ks[13], (DEPTH, CONV_K, D_CONV), CONV_K ** -0.5),
        'conv_b': nrm(ks[14], (DEPTH, D_CONV), 0.02),
        'conv_norm_g': 1.0 + nrm(ks[15], (DEPTH, D_CONV), 0.1),
        'conv_norm_b': nrm(ks[16], (DEPTH, D_CONV), 0.02),
        'w_conv_o': nrm(ks[17], (DEPTH, D_CONV, D_MODEL), D_CONV ** -0.5),
        'w_out': nrm(ks[18], (DEPTH, D_MODEL, D_MODEL), D_MODEL ** -0.5),
        'g_ffn': 1.0 + nrm(ks[19], (DEPTH, D_MODEL), 0.1),
        'w_ffn_in': nrm(ks[20], (DEPTH, D_MODEL, 2 * D_FF), D_MODEL ** -0.5),
        'w_ffn_o': nrm(ks[21], (DEPTH, D_FF, D_MODEL), D_FF ** -0.5),
        'g_final': 1.0 + nrm(ks[22], (D_MODEL,), 0.1),
    }


def reference(x_prompt, x_sample, cache_k, cache_v, state_conv, page_table, c_prompt, c_sample,
              w_ada, b_ada, g_mix, w_in, w_attn_o, conv_w, conv_b, conv_norm_g, conv_norm_b,
              w_conv_o, w_out, g_ffn, w_ffn_in, w_ffn_o, g_final):
    seq = x_prompt.shape[1]
    past_len = page_table.shape[1] * PAGE_SIZE
    pos_p = jnp.arange(seq, dtype=jnp.int32)
    pos_s = past_len + jnp.arange(x_sample.shape[1], dtype=jnp.int32)
    left_p = jnp.zeros((x_prompt.shape[0], CONV_K - 1, D_CONV), x_prompt.dtype)
    xp, xs = x_prompt, x_sample
    kp, vp, cp, ksm, vsm, csm = [], [], [], [], [], []
    for l in range(DEPTH):
        lw = (w_ada[l], b_ada[l], g_mix[l], w_in[l], w_attn_o[l], conv_w[l], conv_b[l],
              conv_norm_g[l], conv_norm_b[l], w_conv_o[l], w_out[l], g_ffn[l],
              w_ffn_in[l], w_ffn_o[l])
        xp, k_new, v_new, c_new = layer_forward(xp, c_prompt, pos_p, left_p, moba_prompt, *lw)
        kp.append(k_new)
        vp.append(v_new)
        cp.append(c_new)
        attend_s = functools.partial(moba_sample, cache_k, cache_v, l, page_table)
        xs, k_new, v_new, c_new = layer_forward(xs, c_sample, pos_s, state_conv[l], attend_s, *lw)
        ksm.append(k_new)
        vsm.append(v_new)
        csm.append(c_new)
    y_prompt = rms_norm(xp, g_final)
    y_sample = rms_norm(xs, g_final)
    return (y_prompt, y_sample, jnp.stack(kp), jnp.stack(vp), jnp.stack(cp),
            jnp.stack(ksm), jnp.stack(vsm), jnp.stack(csm))
```

```python
import functools

import jax
import jax.numpy as jnp
from jax import lax
from jax.experimental import pallas as pl
from jax.experimental.pallas import tpu as pltpu

D_MODEL = 1024
N_HEADS = 8
HEAD_DIM = 64
D_ATTN = N_HEADS * HEAD_DIM
D_CONV = D_MODEL // 2
CONV_K = 31
CONV_HALO = 32
MOBA_BLOCK = 256
MOBA_TOPK = 3
PAGE_SIZE = 128
PAGES_PER_BLOCK = MOBA_BLOCK // PAGE_SIZE
ROPE_THETA = 10000.0
D_FF = 2816
EPS = 1e-6
D_IN = 3 * D_ATTN + 2 * D_CONV + 2 * D_MODEL
LANES = 128
HEADS_PER_TILE = LANES // HEAD_DIM
N_HEAD_PAIRS = N_HEADS // HEADS_PER_TILE
FFN_CHUNK = 256
VMEM_LIMIT = 56 * 1024 * 1024

F32 = jnp.float32
BF16 = jnp.bfloat16
NEG_INF = float("-inf")


def _resident(shape):
    return pl.BlockSpec(shape, lambda *_: (0,) * len(shape), pipeline_mode=pl.Buffered(1))


def _sigmoid(x):
    return 1.0 / (1.0 + jnp.exp(-x))


def _silu(x):
    return x * _sigmoid(x)


def _rms_mod(x, g, sc, sh):
    y = x * lax.rsqrt(jnp.mean(x * x, axis=-1, keepdims=True) + EPS) * g
    return y * (1.0 + sc) + sh


def _rope(y, cos, sin):
    reps = D_ATTN // LANES
    cos = jnp.concatenate([cos] * reps, axis=-1)
    sin = jnp.concatenate([sin] * reps, axis=-1)
    half = HEAD_DIM // 2
    lane = lax.broadcasted_iota(jnp.int32, y.shape, 1)
    first = (lane % HEAD_DIM) < half
    partner = jnp.where(first, pltpu.roll(y, D_ATTN - half, 1), pltpu.roll(y, half, 1))
    return y * cos + partner * sin


def _layer_norm_silu(c, g, b):
    mu = jnp.mean(c, axis=-1, keepdims=True)
    d = c - mu
    y = d * lax.rsqrt(jnp.mean(d * d, axis=-1, keepdims=True) + EPS) * g + b
    return _silu(y)


def _ada_kernel(c_ref, w_ref, b_ref, o_ref):
    c = c_ref[...]
    o_ref[0] = jnp.dot(_silu(c).astype(BF16), w_ref[0].astype(BF16), preferred_element_type=F32) + b_ref[0]


def _ada(c_all, w_ada, b_ada):
    depth, _, n_out = w_ada.shape
    rows = c_all.shape[0]
    tn = 1024
    return pl.pallas_call(
        _ada_kernel,
        out_shape=jax.ShapeDtypeStruct((depth, rows, n_out), F32),
        grid=(depth, n_out // tn),
        in_specs=[pl.BlockSpec((rows, D_MODEL), lambda l, j: (0, 0)),
                  pl.BlockSpec((1, D_MODEL, tn), lambda l, j: (l, 0, j)),
                  pl.BlockSpec((1, 1, tn), lambda l, j: (l, 0, j))],
        out_specs=pl.BlockSpec((1, rows, tn), lambda l, j: (l, 0, j)),
        compiler_params=pltpu.CompilerParams(dimension_semantics=("arbitrary", "arbitrary"),
                                             vmem_limit_bytes=VMEM_LIMIT),
        name="ada_mod",
    )(c_all, w_ada, b_ada.reshape(depth, 1, n_out))


def _inproj_prompt_kernel(x_ref, sh_ref, sc_ref, g_ref, w_ref, cos_ref, sin_ref, cw_ref, cb_ref, lng_ref, lnb_ref,
                          q_ref, k_ref, v_ref, kb_ref, vb_ref, km_ref, cbn_ref, ga_ref, gb_ref, cs_ref,
                          ubuf, *, tm, nt):
    t = pl.program_id(1)
    hb = _rms_mod(x_ref[0], g_ref[...], sc_ref[0, 0], sh_ref[0, 0]).astype(BF16)

    def proj(lo, hi):
        return jnp.dot(hb, w_ref[:, lo:hi], preferred_element_type=F32)

    cos, sin = cos_ref[...], sin_ref[...]
    q_ref[0] = _rope(proj(0, D_ATTN), cos, sin)
    k = _rope(proj(D_ATTN, 2 * D_ATTN), cos, sin)
    k_ref[0] = k
    kb_ref[0] = k.astype(BF16)
    for j in range(tm // MOBA_BLOCK):
        km_ref[0, j] = jnp.sum(k[j * MOBA_BLOCK:(j + 1) * MOBA_BLOCK], axis=0, keepdims=True) * (1.0 / MOBA_BLOCK)
    v = proj(2 * D_ATTN, 3 * D_ATTN)
    v_ref[0] = v
    vb_ref[0] = v.astype(BF16)
    u0 = 3 * D_ATTN
    glu = proj(u0, u0 + D_CONV) * _sigmoid(proj(u0 + D_CONV, u0 + 2 * D_CONV))
    g0 = u0 + 2 * D_CONV
    ga_ref[0] = _sigmoid(proj(g0, g0 + D_MODEL)).astype(BF16)
    gb_ref[0] = _sigmoid(proj(g0 + D_MODEL, g0 + 2 * D_MODEL)).astype(BF16)

    @pl.when(t == 0)
    def _():
        ubuf[0:CONV_HALO, :] = jnp.zeros((CONV_HALO, D_CONV), F32)

    ubuf[CONV_HALO:CONV_HALO + tm, :] = glu
    acc = jnp.zeros((tm, D_CONV), F32) + cb_ref[...]
    for d in range(CONV_K):
        acc = acc + cw_ref[pl.ds(CONV_K - 1 - d, 1), :] * ubuf[pl.ds(CONV_HALO - d, tm), :]
    cbn_ref[0] = _layer_norm_silu(acc, lng_ref[...], lnb_ref[...]).astype(BF16)

    @pl.when(t == nt - 1)
    def _():
        cs_ref[0] = ubuf[pl.ds(CONV_HALO + tm - (CONV_K - 1), CONV_K - 1), :]

    ubuf[0:CONV_HALO, :] = ubuf[tm:tm + CONV_HALO, :]


def _inproj_prompt(x, mod4, l, g_mix, w_in, cos, sin, conv_w, conv_b, ln_g, ln_b, *, tm):
    bsz, s, _ = x.shape
    nt = s // tm
    nb = s // MOBA_BLOCK
    row = lambda b, t: (b, t, 0)
    vec = lambda n: pl.BlockSpec((1, n), lambda b, t: (0, 0))
    mod = lambda j: pl.BlockSpec((1, 1, 1, D_MODEL), lambda b, t: (l, b, 0, j))
    tok = lambda n: pl.BlockSpec((1, tm, n), row)
    out_shape = (
        jax.ShapeDtypeStruct((bsz, s, D_ATTN), F32),
        jax.ShapeDtypeStruct((bsz, s, D_ATTN), F32),
        jax.ShapeDtypeStruct((bsz, s, D_ATTN), F32),
        jax.ShapeDtypeStruct((bsz, s, D_ATTN), BF16),
        jax.ShapeDtypeStruct((bsz, s, D_ATTN), BF16),
        jax.ShapeDtypeStruct((bsz, nb, 1, D_ATTN), F32),
        jax.ShapeDtypeStruct((bsz, s, D_CONV), BF16),
        jax.ShapeDtypeStruct((bsz, s, D_MODEL), BF16),
        jax.ShapeDtypeStruct((bsz, s, D_MODEL), BF16),
        jax.ShapeDtypeStruct((bsz, CONV_K - 1, D_CONV), F32),
    )
    out_specs = (tok(D_ATTN), tok(D_ATTN), tok(D_ATTN), tok(D_ATTN), tok(D_ATTN),
                 pl.BlockSpec((1, tm // MOBA_BLOCK, 1, D_ATTN), lambda b, t: (b, t, 0, 0)),
                 tok(D_CONV), tok(D_MODEL), tok(D_MODEL),
                 pl.BlockSpec((1, CONV_K - 1, D_CONV), lambda b, t: (b, 0, 0)))
    return pl.pallas_call(
        functools.partial(_inproj_prompt_kernel, tm=tm, nt=nt),
        out_shape=out_shape,
        grid=(bsz, nt),
        in_specs=[tok(D_MODEL), mod(0), mod(1), vec(D_MODEL), _resident((D_MODEL, D_IN)),
                  pl.BlockSpec((tm, LANES), lambda b, t: (t, 0)), pl.BlockSpec((tm, LANES), lambda b, t: (t, 0)),
                  pl.BlockSpec((CONV_K, D_CONV), lambda b, t: (0, 0)), vec(D_CONV), vec(D_CONV), vec(D_CONV)],
        out_specs=out_specs,
        scratch_shapes=[pltpu.VMEM((CONV_HALO + tm, D_CONV), F32)],
        compiler_params=pltpu.CompilerParams(dimension_semantics=("arbitrary", "arbitrary"),
                                             vmem_limit_bytes=VMEM_LIMIT),
        name="inproj_prompt",
    )(x, mod4, mod4, g_mix, w_in, cos, sin, conv_w, conv_b, ln_g, ln_b)


def _head_rows_mask(rows):
    lane_head = lax.broadcasted_iota(jnp.int32, (rows, D_ATTN), 1) // HEAD_DIM
    return lane_head == lax.broadcasted_iota(jnp.int32, (rows, D_ATTN), 0) % N_HEADS


def _attn_prompt_kernel(q_ref, k_ref, v_ref, km_ref, o_ref, qm_ref, sel_ref, *, nb):
    qb = pl.program_id(1)
    tq = MOBA_BLOCK
    q = q_ref[0]

    hm = _head_rows_mask(N_HEADS)
    km = km_ref[0]
    kmt = jnp.concatenate([jnp.where(hm, km[n:n + 1, :], 0.0) for n in range(nb)], axis=0)
    gate = lax.dot_general(q, kmt, (((1,), (1,)), ((), ())), precision=lax.Precision.HIGHEST,
                           preferred_element_type=F32)
    g = [gate[:, n * N_HEADS:(n + 1) * N_HEADS] for n in range(nb)]
    for n in range(nb - 1):
        rank = jnp.zeros((tq, N_HEADS), F32)
        for m in range(nb - 1):
            if m != n:
                beats = (g[m] >= g[n]) if m < n else (g[m] > g[n])
                rank = rank + jnp.where(beats, (m < qb).astype(F32), 0.0)
        sel_ref[n, :, 0:N_HEADS] = jnp.where(rank < MOBA_TOPK, 1.0, 0.0)

    scale = HEAD_DIM ** -0.5
    lo = lax.broadcasted_iota(jnp.int32, (tq, LANES), 1) < HEAD_DIM
    for p in range(N_HEAD_PAIRS):
        q2 = q[:, p * LANES:(p + 1) * LANES] * scale
        qm_ref[2 * p] = jnp.where(lo, q2, 0.0).astype(BF16)
        qm_ref[2 * p + 1] = jnp.where(lo, 0.0, q2).astype(BF16)

    def scores(h, kk):
        return lax.dot_general(qm_ref[h], kk, (((1,), (1,)), ((), ())), preferred_element_type=F32)

    row = lax.broadcasted_iota(jnp.int32, (tq, tq), 0)
    col = lax.broadcasted_iota(jnp.int32, (tq, tq), 1)
    causal = row >= col
    own = pl.multiple_of(qb * MOBA_BLOCK, MOBA_BLOCK)
    m0, l0, acc0 = [], [], []
    for p in range(N_HEAD_PAIRS):
        kk = k_ref[0, pl.ds(own, MOBA_BLOCK), p * LANES:(p + 1) * LANES]
        vv = v_ref[0, pl.ds(own, MOBA_BLOCK), p * LANES:(p + 1) * LANES]
        pv = []
        for hh in range(HEADS_PER_TILE):
            s = jnp.where(causal, scores(2 * p + hh, kk), NEG_INF)
            m = jnp.max(s, axis=-1, keepdims=True)
            e = jnp.exp(s - m)
            m0.append(m)
            l0.append(jnp.sum(e, axis=-1, keepdims=True))
            pv.append(jnp.dot(e.astype(BF16), vv, preferred_element_type=F32))
        acc0.append(jnp.where(lo, pv[0], pv[1]))

    def past_block(n, carry):
        ms, ls, accs = carry
        ms, ls, accs = list(ms), list(ls), list(accs)
        start = pl.multiple_of(n * MOBA_BLOCK, MOBA_BLOCK)
        for p in range(N_HEAD_PAIRS):
            kk = k_ref[0, pl.ds(start, MOBA_BLOCK), p * LANES:(p + 1) * LANES]
            vv = v_ref[0, pl.ds(start, MOBA_BLOCK), p * LANES:(p + 1) * LANES]
            pv, alpha = [], []
            for hh in range(HEADS_PER_TILE):
                h = 2 * p + hh
                s = jnp.where(sel_ref[n, :, h:h + 1] > 0.5, scores(h, kk), NEG_INF)
                m_new = jnp.maximum(ms[h], jnp.max(s, axis=-1, keepdims=True))
                a = jnp.exp(ms[h] - m_new)
                e = jnp.exp(s - m_new)
                ls[h] = a * ls[h] + jnp.sum(e, axis=-1, keepdims=True)
                ms[h] = m_new
                alpha.append(a)
                pv.append(jnp.dot(e.astype(BF16), vv, preferred_element_type=F32))
            accs[p] = jnp.where(lo, alpha[0] * accs[p] + pv[0], alpha[1] * accs[p] + pv[1])
        return tuple(ms), tuple(ls), tuple(accs)

    _, ls, accs = lax.fori_loop(0, qb, past_block, (tuple(m0), tuple(l0), tuple(acc0)))
    out = [accs[p] / jnp.where(lo, ls[2 * p], ls[2 * p + 1]) for p in range(N_HEAD_PAIRS)]
    o_ref[0] = jnp.concatenate(out, axis=-1).astype(BF16)


def _attn_prompt(q, kb, vb, kmean):
    bsz, s, _ = q.shape
    nb = s // MOBA_BLOCK
    return pl.pallas_call(
        functools.partial(_attn_prompt_kernel, nb=nb),
        out_shape=jax.ShapeDtypeStruct((bsz, s, D_ATTN), BF16),
        grid=(bsz, nb),
        in_specs=[pl.BlockSpec((1, MOBA_BLOCK, D_ATTN), lambda b, i: (b, i, 0)),
                  pl.BlockSpec((1, s, D_ATTN), lambda b, i: (b, 0, 0)),
                  pl.BlockSpec((1, s, D_ATTN), lambda b, i: (b, 0, 0)),
                  pl.BlockSpec((1, nb, D_ATTN), lambda b, i: (b, 0, 0))],
        out_specs=pl.BlockSpec((1, MOBA_BLOCK, D_ATTN), lambda b, i: (b, i, 0)),
        scratch_shapes=[pltpu.VMEM((N_HEADS, MOBA_BLOCK, LANES), BF16),
                        pltpu.VMEM((max(nb - 1, 1), MOBA_BLOCK, LANES), F32)],
        compiler_params=pltpu.CompilerParams(dimension_semantics=("arbitrary", "arbitrary"),
                                             vmem_limit_bytes=VMEM_LIMIT),
        name="attn_prompt",
    )(q, kb, vb, kmean)


def _mlp_kernel(x_ref, a_ref, cbn_ref, ga_ref, gb_ref, gtm_ref, shf_ref, scf_ref, gtf_ref, gffn_ref, gfin_ref,
                wao_ref, wco_ref, wout_ref, wfi_ref, wfo_ref, o_ref, *, final):
    a = jnp.dot(a_ref[...].astype(BF16), wao_ref[...], preferred_element_type=F32)
    cb = jnp.dot(cbn_ref[...], wco_ref[...], preferred_element_type=F32)
    merged = ga_ref[...].astype(F32) * a + gb_ref[...].astype(F32) * cb
    x = x_ref[...] + gtm_ref[...] * jnp.dot(merged.astype(BF16), wout_ref[...], preferred_element_type=F32)
    hb = _rms_mod(x, gffn_ref[...], scf_ref[...], shf_ref[...]).astype(BF16)
    acc = jnp.zeros(x.shape, F32)
    for c in range(0, D_FF, FFN_CHUNK):
        gate = jnp.dot(hb, wfi_ref[:, c:c + FFN_CHUNK], preferred_element_type=F32)
        up = jnp.dot(hb, wfi_ref[:, D_FF + c:D_FF + c + FFN_CHUNK], preferred_element_type=F32)
        acc = acc + jnp.dot((_silu(gate) * up).astype(BF16), wfo_ref[c:c + FFN_CHUNK, :], preferred_element_type=F32)
    x = x + gtf_ref[...] * acc
    if final:
        x = x * lax.rsqrt(jnp.mean(x * x, axis=-1, keepdims=True) + EPS) * gfin_ref[...]
    o_ref[...] = x


def _mlp(x, attn, cbn, ga, gb, mod_rows, g_ffn, g_final, w_attn_o, w_conv_o, w_out, w_ffn_in, w_ffn_o,
         *, tm, rows_per_mod, final):
    m = x.shape[0]
    tok = lambda n: pl.BlockSpec((tm, n), lambda i: (i, 0))
    vec = lambda n: pl.BlockSpec((1, n), lambda i: (0, 0))
    if rows_per_mod == 1:
        mod = lambda j: pl.BlockSpec((tm, D_MODEL), lambda i: (i, j))
    else:
        assert rows_per_mod % tm == 0
        mod_rows = mod_rows.reshape(mod_rows.shape[0], 1, 6 * D_MODEL)
        mod = lambda j: pl.BlockSpec((None, 1, D_MODEL), lambda i: (i // (rows_per_mod // tm), 0, j))
    return pl.pallas_call(
        functools.partial(_mlp_kernel, final=final),
        out_shape=jax.ShapeDtypeStruct((m, D_MODEL), F32),
        grid=(m // tm,),
        in_specs=[tok(D_MODEL), tok(D_ATTN), tok(D_CONV), tok(D_MODEL), tok(D_MODEL),
                  mod(2), mod(3), mod(4), mod(5), vec(D_MODEL), vec(D_MODEL),
                  _resident((D_ATTN, D_MODEL)), _resident((D_CONV, D_MODEL)), _resident((D_MODEL, D_MODEL)),
                  _resident((D_MODEL, 2 * D_FF)), _resident((D_FF, D_MODEL))],
        out_specs=tok(D_MODEL),
        compiler_params=pltpu.CompilerParams(dimension_semantics=("arbitrary",), vmem_limit_bytes=VMEM_LIMIT),
        name="mixer_out_ffn",
    )(x, attn, cbn, ga, gb, mod_rows, mod_rows, mod_rows, mod_rows, g_ffn, g_final,
      w_attn_o, w_conv_o, w_out, w_ffn_in, w_ffn_o)


def _inproj_sample_kernel(x_ref, sh_ref, sc_ref, g_ref, w_ref, cos_ref, sin_ref, st_ref, cw_ref, cb_ref, lng_ref,
                          lnb_ref, q_ref, k_ref, v_ref, cbn_ref, ga_ref, gb_ref, cs_ref, uc, *, db, t_new):
    hb = _rms_mod(x_ref[...], g_ref[...], sc_ref[...], sh_ref[...]).astype(BF16)

    def proj(lo, hi):
        return jnp.dot(hb, w_ref[:, lo:hi], preferred_element_type=F32)

    cos, sin = cos_ref[...], sin_ref[...]
    q_ref[...] = _rope(proj(0, D_ATTN), cos, sin)
    k_ref[...] = _rope(proj(D_ATTN, 2 * D_ATTN), cos, sin)
    v_ref[...] = proj(2 * D_ATTN, 3 * D_ATTN)
    u0 = 3 * D_ATTN
    glu = proj(u0, u0 + D_CONV) * _sigmoid(proj(u0 + D_CONV, u0 + 2 * D_CONV))
    g0 = u0 + 2 * D_CONV
    ga_ref[...] = _sigmoid(proj(g0, g0 + D_MODEL)).astype(BF16)
    gb_ref[...] = _sigmoid(proj(g0 + D_MODEL, g0 + 2 * D_MODEL)).astype(BF16)

    ctx = CONV_K - 1
    uc[:, 0:ctx, :] = st_ref[...]
    uc[:, ctx:ctx + t_new, :] = glu.reshape(db, t_new, D_CONV)
    acc = jnp.zeros((db, t_new, D_CONV), F32) + cb_ref[...]
    for j in range(CONV_K):
        acc = acc + cw_ref[pl.ds(j, 1), :] * uc[:, pl.ds(j, t_new), :]
    cbn_ref[...] = _layer_norm_silu(acc.reshape(db * t_new, D_CONV), lng_ref[...], lnb_ref[...]).astype(BF16)
    cs_ref[...] = uc[:, pl.ds(t_new, ctx), :]


def _inproj_sample(x, sh, sc, g_mix, w_in, cos, sin, state, conv_w, conv_b, ln_g, ln_b, *, db, t_new):
    m = db * t_new
    full = lambda *shape: pl.BlockSpec(shape, lambda i: (0,) * len(shape))
    out_shape = (
        jax.ShapeDtypeStruct((m, D_ATTN), F32), jax.ShapeDtypeStruct((m, D_ATTN), F32),
        jax.ShapeDtypeStruct((m, D_ATTN), F32), jax.ShapeDtypeStruct((m, D_CONV), BF16),
        jax.ShapeDtypeStruct((m, D_MODEL), BF16), jax.ShapeDtypeStruct((m, D_MODEL), BF16),
        jax.ShapeDtypeStruct((db, CONV_K - 1, D_CONV), F32),
    )
    ctx_rows = -(-(CONV_K - 1 + t_new) // 8) * 8
    return pl.pallas_call(
        functools.partial(_inproj_sample_kernel, db=db, t_new=t_new),
        out_shape=out_shape,
        grid=(1,),
        in_specs=[full(m, D_MODEL), pl.BlockSpec((m, D_MODEL), lambda i: (0, 0)),
                  pl.BlockSpec((m, D_MODEL), lambda i: (0, 1)), full(1, D_MODEL), full(D_MODEL, D_IN),
                  full(m, LANES), full(m, LANES), full(db, CONV_K - 1, D_CONV), full(CONV_K, D_CONV),
                  full(1, D_CONV), full(1, D_CONV), full(1, D_CONV)],
        out_specs=(full(m, D_ATTN), full(m, D_ATTN), full(m, D_ATTN), full(m, D_CONV), full(m, D_MODEL),
                   full(m, D_MODEL), full(db, CONV_K - 1, D_CONV)),
        scratch_shapes=[pltpu.VMEM((db, ctx_rows, D_CONV), F32)],
        compiler_params=pltpu.CompilerParams(dimension_semantics=("arbitrary",), vmem_limit_bytes=VMEM_LIMIT),
        name="inproj_sample",
    )(x, sh, sc, g_mix, w_in, cos, sin, state, conv_w, conv_b, ln_g, ln_b)


def _page_copy(cache_ref, layer, page, dst_ref, sem):
    return pltpu.make_async_copy(cache_ref.at[layer, page], dst_ref, sem)


def _select_kernel(pt_ref, q_ref, ck_ref, idx_ref, buf, km, sems, *, layer, n_pages, chunk, t_new):
    b = pl.program_id(0)
    n_chunks = n_pages // chunk

    def copies(c, slot):
        return [_page_copy(ck_ref, layer, pt_ref[b * n_pages + c * chunk + g], buf.at[slot, g], sems.at[slot])
                for g in range(chunk)]

    for cp in copies(0, 0):
        cp.start()
    for c in range(n_chunks):
        slot = c % 2
        if c + 1 < n_chunks:
            for cp in copies(c + 1, 1 - slot):
                cp.start()
        for cp in copies(c, slot):
            cp.wait()
        for j in range(chunk // PAGES_PER_BLOCK):
            tot = jnp.zeros((1, D_ATTN), F32)
            for g in range(PAGES_PER_BLOCK):
                tot = tot + jnp.sum(buf[slot, j * PAGES_PER_BLOCK + g], axis=0, keepdims=True)
            blk = c * (chunk // PAGES_PER_BLOCK) + j
            km[blk:blk + 1, :] = tot * (1.0 / MOBA_BLOCK)

    n_blk = n_pages // PAGES_PER_BLOCK
    rows = N_HEADS * t_new
    q = q_ref[...]
    lane_head = lax.broadcasted_iota(jnp.int32, (t_new, D_ATTN), 1) // HEAD_DIM
    qbd = jnp.concatenate([jnp.where(lane_head == h, q, 0.0) for h in range(N_HEADS)], axis=0)
    gate = lax.dot_general(qbd, km[...], (((1,), (1,)), ((), ())), precision=lax.Precision.HIGHEST,
                           preferred_element_type=F32)
    col = lax.broadcasted_iota(jnp.int32, (rows, n_blk), 1)
    out_lane = lax.broadcasted_iota(jnp.int32, (rows, LANES), 1)
    out = jnp.zeros((rows, LANES), jnp.int32)
    for r in range(MOBA_TOPK):
        best = jnp.max(gate, axis=-1, keepdims=True)
        idx = jnp.min(jnp.where(gate == best, col, n_blk), axis=-1, keepdims=True)
        out = jnp.where(out_lane == r, idx, out)
        gate = jnp.where(col == idx, NEG_INF, gate)
    idx_ref[0] = out


def _select_blocks(page_table, q, cache_k, *, layer, db, t_new):
    n_pages = page_table.shape[1]
    n_blk = n_pages // PAGES_PER_BLOCK
    chunk = 16
    assert n_pages % chunk == 0 and chunk % PAGES_PER_BLOCK == 0
    rows = N_HEADS * t_new
    grid_spec = pltpu.PrefetchScalarGridSpec(
        num_scalar_prefetch=1,
        grid=(db,),
        in_specs=[pl.BlockSpec((t_new, D_ATTN), lambda b, pt: (b, 0)),
                  pl.BlockSpec(memory_space=pl.ANY)],
        out_specs=pl.BlockSpec((1, rows, LANES), lambda b, pt: (b, 0, 0)),
        scratch_shapes=[pltpu.VMEM((2, chunk, PAGE_SIZE, D_ATTN), F32),
                        pltpu.VMEM((n_blk, D_ATTN), F32),
                        pltpu.SemaphoreType.DMA((2,))],
    )
    return pl.pallas_call(
        functools.partial(_select_kernel, layer=layer, n_pages=n_pages, chunk=chunk, t_new=t_new),
        out_shape=jax.ShapeDtypeStruct((db, rows, LANES), jnp.int32),
        grid_spec=grid_spec,
        compiler_params=pltpu.CompilerParams(dimension_semantics=("arbitrary",), vmem_limit_bytes=VMEM_LIMIT),
        name="select_blocks",
    )(page_table.reshape(-1), q, cache_k)


def _attn_sample_kernel(sel_ref, pt_ref, q_ref, kn_ref, vn_ref, ck_ref, cv_ref, o_ref, kbuf, vbuf, sems,
                        *, layer, n_pages, t_new):
    b = pl.program_id(0)
    n_slots = HEADS_PER_TILE * t_new * MOBA_TOPK
    scale = HEAD_DIM ** -0.5
    lane = lax.broadcasted_iota(jnp.int32, (1, LANES), 1)
    tri = lax.broadcasted_iota(jnp.int32, (t_new, 1), 0)

    for p in range(N_HEAD_PAIRS):
        lanes = pl.ds(p * LANES, LANES)

        def copies(slot):
            hh = slot // (t_new * MOBA_TOPK)
            rest = slot - hh * (t_new * MOBA_TOPK)
            blk = sel_ref[((b * N_HEADS + HEADS_PER_TILE * p + hh) * t_new) * MOBA_TOPK + rest]
            out = []
            for g in range(PAGES_PER_BLOCK):
                page = pt_ref[b * n_pages + blk * PAGES_PER_BLOCK + g]
                rows = pl.ds(g * PAGE_SIZE, PAGE_SIZE)
                out.append(pltpu.make_async_copy(ck_ref.at[layer, page, :, lanes], kbuf.at[slot, rows, :], sems.at[0]))
                out.append(pltpu.make_async_copy(cv_ref.at[layer, page, :, lanes], vbuf.at[slot, rows, :], sems.at[1]))
            return out

        def start(slot, carry):
            for cp in copies(slot):
                cp.start()
            return carry

        def wait(slot, carry):
            for cp in copies(slot):
                cp.wait()
            return carry

        lax.fori_loop(0, n_slots, start, 0)
        lax.fori_loop(0, n_slots, wait, 0)

        kn = kn_ref[:, lanes]
        vn = vn_ref[:, lanes]

        q_pair = q_ref[:, lanes] * scale

        def query(t, out):
            q2 = jnp.sum(jnp.where(tri == t, q_pair, 0.0), axis=0, keepdims=True)
            res = []
            for hh in range(HEADS_PER_TILE):
                head = (lane >= hh * HEAD_DIM) & (lane < (hh + 1) * HEAD_DIM)
                qh = jnp.where(head, q2, 0.0)
                base = (hh * t_new + t) * MOBA_TOPK
                s_own = jnp.where(tri <= t, jnp.sum(kn * qh, axis=-1, keepdims=True), NEG_INF)
                s = [jnp.sum(kbuf[base + r] * qh, axis=-1, keepdims=True) for r in range(MOBA_TOPK)]
                m = jnp.max(s_own, axis=0, keepdims=True)
                for sr in s:
                    m = jnp.maximum(m, jnp.max(sr, axis=0, keepdims=True))
                e_own = jnp.exp(s_own - m)
                den = jnp.sum(e_own, axis=0, keepdims=True)
                num = jnp.sum(e_own * vn, axis=0, keepdims=True)
                for r in range(MOBA_TOPK):
                    e = jnp.exp(s[r] - m)
                    den = den + jnp.sum(e, axis=0, keepdims=True)
                    num = num + jnp.sum(e * vbuf[base + r], axis=0, keepdims=True)
                res.append(num / den)
            return jnp.where(tri == t, jnp.where(lane < HEAD_DIM, res[0], res[1]), out)

        o_ref[:, lanes] = lax.fori_loop(0, t_new, query, jnp.zeros((t_new, LANES), F32))


def _attn_sample(sel, page_table, q, k_new, v_new, cache_k, cache_v, *, layer, db, t_new):
    n_pages = page_table.shape[1]
    n_slots = HEADS_PER_TILE * t_new * MOBA_TOPK
    tok = pl.BlockSpec((t_new, D_ATTN), lambda b, s, pt: (b, 0))
    grid_spec = pltpu.PrefetchScalarGridSpec(
        num_scalar_prefetch=2,
        grid=(db,),
        in_specs=[tok, tok, tok, pl.BlockSpec(memory_space=pl.ANY), pl.BlockSpec(memory_space=pl.ANY)],
        out_specs=tok,
        scratch_shapes=[pltpu.VMEM((n_slots, MOBA_BLOCK, LANES), F32),
                        pltpu.VMEM((n_slots, MOBA_BLOCK, LANES), F32),
                        pltpu.SemaphoreType.DMA((2,))],
    )
    return pl.pallas_call(
        functools.partial(_attn_sample_kernel, layer=layer, n_pages=n_pages, t_new=t_new),
        out_shape=jax.ShapeDtypeStruct((db * t_new, D_ATTN), F32),
        grid_spec=grid_spec,
        compiler_params=pltpu.CompilerParams(dimension_semantics=("arbitrary",), vmem_limit_bytes=VMEM_LIMIT),
        name="attn_sample",
    )(sel.reshape(-1), page_table.reshape(-1), q, k_new, v_new, cache_k, cache_v)


def _rope_tables(pos):
    half = HEAD_DIM // 2
    inv_freq = ROPE_THETA ** (-jnp.arange(half, dtype=F32) / half)
    ang = pos.astype(F32)[:, None] * inv_freq[None, :]
    cos, sin = jnp.cos(ang), jnp.sin(ang)
    cos = jnp.concatenate([cos, cos] * HEADS_PER_TILE, axis=-1)
    sin = jnp.concatenate([-sin, sin] * HEADS_PER_TILE, axis=-1)
    return cos, sin


def kernel(x_prompt, x_sample, cache_k, cache_v, state_conv, page_table, c_prompt, c_sample, w_ada, b_ada, g_mix, w_in, w_attn_o, conv_w, conv_b, conv_norm_g, conv_norm_b, w_conv_o, w_out, g_ffn, w_ffn_in, w_ffn_o, g_final):
    bsz, seq, _ = x_prompt.shape
    db, t_new, _ = x_sample.shape
    depth = w_in.shape[0]
    n_pages = page_table.shape[1]
    past = n_pages * PAGE_SIZE
    n_phys = cache_k.shape[1]
    assert seq % MOBA_BLOCK == 0 and past % MOBA_BLOCK == 0 and t_new <= MOBA_BLOCK and t_new % 8 == 0
    assert past // MOBA_BLOCK >= MOBA_TOPK
    tm = 512 if seq % 512 == 0 else MOBA_BLOCK

    cos_p, sin_p = _rope_tables(jnp.arange(seq, dtype=jnp.int32))
    cos_s, sin_s = _rope_tables(past + jnp.arange(t_new, dtype=jnp.int32))
    cos_s, sin_s = jnp.tile(cos_s, (db, 1)), jnp.tile(sin_s, (db, 1))

    mod = _ada(jnp.concatenate([c_prompt, c_sample], axis=0), w_ada, b_ada)
    mod4 = mod.reshape(depth, bsz + db, 1, 6 * D_MODEL)
    ck = cache_k.reshape(depth, n_phys, PAGE_SIZE, D_ATTN)
    cv = cache_v.reshape(depth, n_phys, PAGE_SIZE, D_ATTN)
    w_in_b, w_attn_o_b, w_conv_o_b = w_in.astype(BF16), w_attn_o.astype(BF16), w_conv_o.astype(BF16)
    w_out_b, w_ffn_in_b, w_ffn_o_b = w_out.astype(BF16), w_ffn_in.astype(BF16), w_ffn_o.astype(BF16)
    row2 = lambda a, l: a[l][None, :]
    g_fin = g_final[None, :]

    xp = x_prompt
    xs = x_sample.reshape(db * t_new, D_MODEL)
    kp, vp, cp, ksm, vsm, csm = [], [], [], [], [], []
    for l in range(depth):
        final = l == depth - 1
        lw = (row2(g_mix, l), w_in_b[l])
        cw = (conv_w[l], row2(conv_b, l), row2(conv_norm_g, l), row2(conv_norm_b, l))
        mw = (row2(g_ffn, l), g_fin, w_attn_o_b[l], w_conv_o_b[l], w_out_b[l], w_ffn_in_b[l], w_ffn_o_b[l])

        q, k, v, kb, vb, kmean, cbn, ga, gb, cs = _inproj_prompt(xp, mod4, l, *lw, cos_p, sin_p, *cw, tm=tm)
        attn = _attn_prompt(q, kb, vb, kmean.reshape(bsz, seq // MOBA_BLOCK, D_ATTN))
        flat = lambda a: a.reshape(bsz * seq, a.shape[-1])
        xp = _mlp(flat(xp), flat(attn), flat(cbn), flat(ga), flat(gb), mod[l, :bsz], *mw,
                  tm=tm, rows_per_mod=seq, final=final).reshape(bsz, seq, D_MODEL)
        kp.append(k.reshape(bsz, seq, N_HEADS, HEAD_DIM))
        vp.append(v.reshape(bsz, seq, N_HEADS, HEAD_DIM))
        cp.append(cs)

        mod_s = jnp.repeat(mod[l, bsz:], t_new, axis=0)
        qs, ks, vs, cbn_s, ga_s, gb_s, cs_s = _inproj_sample(xs, mod_s, mod_s, *lw, cos_s, sin_s, state_conv[l], *cw,
                                                             db=db, t_new=t_new)
        sel = _select_blocks(page_table, qs, ck, layer=l, db=db, t_new=t_new)[:, :, :MOBA_TOPK]
        attn_s = _attn_sample(sel, page_table, qs, ks, vs, ck, cv, layer=l, db=db, t_new=t_new)
        xs = _mlp(xs, attn_s, cbn_s, ga_s, gb_s, mod_s, *mw, tm=db * t_new, rows_per_mod=1, final=final)
        ksm.append(ks.reshape(db, t_new, N_HEADS, HEAD_DIM))
        vsm.append(vs.reshape(db, t_new, N_HEADS, HEAD_DIM))
        csm.append(cs_s)

    return (xp, xs.reshape(db, t_new, D_MODEL), jnp.stack(kp), jnp.stack(vp), jnp.stack(cp),
            jnp.stack(ksm), jnp.stack(vsm), jnp.stack(csm))
```

```python
import functools

import jax
import jax.numpy as jnp
from jax import lax
from jax.experimental import pallas as pl
from jax.experimental.pallas import tpu as pltpu

D_MODEL = 1024
N_HEADS = 8
HEAD_DIM = 64
D_ATTN = N_HEADS * HEAD_DIM
D_CONV = D_MODEL // 2
CONV_K = 31
CONV_HALO = 32
MOBA_BLOCK = 256
MOBA_TOPK = 3
PAGE_SIZE = 128
PAGES_PER_BLOCK = MOBA_BLOCK // PAGE_SIZE
ROPE_THETA = 10000.0
D_FF = 2816
EPS = 1e-6
D_IN = 3 * D_ATTN + 2 * D_CONV + 2 * D_MODEL
LANES = 128
SUBLANES = 8
HEADS_PER_TILE = LANES // HEAD_DIM
N_HEAD_PAIRS = N_HEADS // HEADS_PER_TILE
FFN_CHUNK = 256
VMEM_LIMIT = 56 * 1024 * 1024

F32 = jnp.float32
BF16 = jnp.bfloat16
NEG_INF = float("-inf")
NT_DIMS = (((1,), (1,)), ((), ()))


def _resident(shape):
    return pl.BlockSpec(shape, lambda *_: (0,) * len(shape), pipeline_mode=pl.Buffered(1))


def _sigmoid(x):
    return 1.0 / (1.0 + jnp.exp(-x))


def _silu(x):
    return x * _sigmoid(x)


def _rms_mod(x, g, sc, sh):
    y = x * lax.rsqrt(jnp.mean(x * x, axis=-1, keepdims=True) + EPS) * g
    return y * (1.0 + sc) + sh


def _rope(y, cos, sin):
    reps = D_ATTN // LANES
    cos = jnp.concatenate([cos] * reps, axis=-1)
    sin = jnp.concatenate([sin] * reps, axis=-1)
    half = HEAD_DIM // 2
    lane = lax.broadcasted_iota(jnp.int32, y.shape, 1)
    first = (lane % HEAD_DIM) < half
    partner = jnp.where(first, pltpu.roll(y, D_ATTN - half, 1), pltpu.roll(y, half, 1))
    return y * cos + partner * sin


def _layer_norm_silu(c, g, b):
    mu = jnp.mean(c, axis=-1, keepdims=True)
    d = c - mu
    y = d * lax.rsqrt(jnp.mean(d * d, axis=-1, keepdims=True) + EPS) * g + b
    return _silu(y)


def _ada_kernel(c_ref, w_ref, b_ref, o_ref):
    c = c_ref[...]
    o_ref[0] = jnp.dot(_silu(c).astype(BF16), w_ref[0].astype(BF16), preferred_element_type=F32) + b_ref[0]


def _ada(c_all, w_ada, b_ada):
    depth, _, n_out = w_ada.shape
    rows = c_all.shape[0]
    tn = 1024
    return pl.pallas_call(
        _ada_kernel,
        out_shape=jax.ShapeDtypeStruct((depth, rows, n_out), F32),
        grid=(depth, n_out // tn),
        in_specs=[pl.BlockSpec((rows, D_MODEL), lambda l, j: (0, 0)),
                  pl.BlockSpec((1, D_MODEL, tn), lambda l, j: (l, 0, j)),
                  pl.BlockSpec((1, 1, tn), lambda l, j: (l, 0, j))],
        out_specs=pl.BlockSpec((1, rows, tn), lambda l, j: (l, 0, j)),
        compiler_params=pltpu.CompilerParams(dimension_semantics=("arbitrary", "arbitrary"),
                                             vmem_limit_bytes=VMEM_LIMIT),
        name="ada_mod",
    )(c_all, w_ada, b_ada.reshape(depth, 1, n_out))


def _inproj_prompt_kernel(x_ref, sh_ref, sc_ref, g_ref, w_ref, cos_ref, sin_ref, cw_ref, cb_ref, lng_ref, lnb_ref,
                          q_ref, k_ref, v_ref, kb_ref, vt_ref, km_ref, cbn_ref, ga_ref, gb_ref, cs_ref,
                          ubuf, zbuf, *, tm, nt):
    t = pl.program_id(1)
    hb = _rms_mod(x_ref[0], g_ref[...], sc_ref[0, 0], sh_ref[0, 0]).astype(BF16)

    def proj(lo, hi):
        return jnp.dot(hb, w_ref[:, lo:hi], preferred_element_type=F32)

    cos, sin = cos_ref[...], sin_ref[...]
    q_ref[0] = _rope(proj(0, D_ATTN), cos, sin)
    k = _rope(proj(D_ATTN, 2 * D_ATTN), cos, sin)
    k_ref[0] = k
    kb_ref[0] = k.astype(BF16)
    for j in range(tm // MOBA_BLOCK):
        km_ref[0, j] = jnp.sum(k[j * MOBA_BLOCK:(j + 1) * MOBA_BLOCK], axis=0, keepdims=True) * (1.0 / MOBA_BLOCK)
    v = proj(2 * D_ATTN, 3 * D_ATTN)
    v_ref[0] = v
    for j in range(tm // MOBA_BLOCK):
        vt_ref[0, j] = v[j * MOBA_BLOCK:(j + 1) * MOBA_BLOCK].T.astype(BF16)
    u0 = 3 * D_ATTN
    glu = proj(u0, u0 + D_CONV) * _sigmoid(proj(u0 + D_CONV, u0 + 2 * D_CONV))
    g0 = u0 + 2 * D_CONV
    ga_ref[0] = _sigmoid(proj(g0, g0 + D_MODEL)).astype(BF16)
    gb_ref[0] = _sigmoid(proj(g0 + D_MODEL, g0 + 2 * D_MODEL)).astype(BF16)

    @pl.when(t == 0)
    def _():
        ubuf[0:CONV_HALO, :] = jnp.zeros((CONV_HALO, D_CONV), F32)

    ubuf[CONV_HALO:CONV_HALO + tm, :] = glu
    acc = jnp.zeros((tm, D_CONV), F32) + cb_ref[...]
    for r in range(SUBLANES):
        z = None
        for a in range(-(-CONV_K // SUBLANES)):
            d = SUBLANES * a + r
            if d < CONV_K:
                term = (cw_ref[pl.ds(CONV_K - 1 - d, 1), :]
                        * ubuf[pl.ds(CONV_HALO - SUBLANES * (a + 1), tm + SUBLANES), :])
                z = term if z is None else z + term
        zbuf[...] = z
        acc = acc + zbuf[pl.ds(SUBLANES - r, tm), :]
    cbn_ref[0] = _layer_norm_silu(acc, lng_ref[...], lnb_ref[...]).astype(BF16)

    @pl.when(t == nt - 1)
    def _():
        cs_ref[0] = ubuf[pl.ds(CONV_HALO + tm - (CONV_K - 1), CONV_K - 1), :]

    ubuf[0:CONV_HALO, :] = ubuf[tm:tm + CONV_HALO, :]


def _inproj_prompt(x, mod4, l, g_mix, w_in, cos, sin, conv_w, conv_b, ln_g, ln_b, *, tm):
    bsz, s, _ = x.shape
    nt = s // tm
    nb = s // MOBA_BLOCK
    row = lambda b, t: (b, t, 0)
    vec = lambda n: pl.BlockSpec((1, n), lambda b, t: (0, 0))
    mod = lambda j: pl.BlockSpec((1, 1, 1, D_MODEL), lambda b, t: (l, b, 0, j))
    tok = lambda n: pl.BlockSpec((1, tm, n), row)
    out_shape = (
        jax.ShapeDtypeStruct((bsz, s, D_ATTN), F32),
        jax.ShapeDtypeStruct((bsz, s, D_ATTN), F32),
        jax.ShapeDtypeStruct((bsz, s, D_ATTN), F32),
        jax.ShapeDtypeStruct((bsz, s, D_ATTN), BF16),
        jax.ShapeDtypeStruct((bsz, nb, D_ATTN, MOBA_BLOCK), BF16),
        jax.ShapeDtypeStruct((bsz, nb, 1, D_ATTN), F32),
        jax.ShapeDtypeStruct((bsz, s, D_CONV), BF16),
        jax.ShapeDtypeStruct((bsz, s, D_MODEL), BF16),
        jax.ShapeDtypeStruct((bsz, s, D_MODEL), BF16),
        jax.ShapeDtypeStruct((bsz, CONV_K - 1, D_CONV), F32),
    )
    out_specs = (tok(D_ATTN), tok(D_ATTN), tok(D_ATTN), tok(D_ATTN),
                 pl.BlockSpec((1, tm // MOBA_BLOCK, D_ATTN, MOBA_BLOCK), lambda b, t: (b, t, 0, 0)),
                 pl.BlockSpec((1, tm // MOBA_BLOCK, 1, D_ATTN), lambda b, t: (b, t, 0, 0)),
                 tok(D_CONV), tok(D_MODEL), tok(D_MODEL),
                 pl.BlockSpec((1, CONV_K - 1, D_CONV), lambda b, t: (b, 0, 0)))
    return pl.pallas_call(
        functools.partial(_inproj_prompt_kernel, tm=tm, nt=nt),
        out_shape=out_shape,
        grid=(bsz, nt),
        in_specs=[tok(D_MODEL), mod(0), mod(1), vec(D_MODEL), _resident((D_MODEL, D_IN)),
                  pl.BlockSpec((tm, LANES), lambda b, t: (t, 0)), pl.BlockSpec((tm, LANES), lambda b, t: (t, 0)),
                  pl.BlockSpec((CONV_K, D_CONV), lambda b, t: (0, 0)), vec(D_CONV), vec(D_CONV), vec(D_CONV)],
        out_specs=out_specs,
        scratch_shapes=[pltpu.VMEM((CONV_HALO + tm, D_CONV), F32), pltpu.VMEM((tm + SUBLANES, D_CONV), F32)],
        compiler_params=pltpu.CompilerParams(dimension_semantics=("arbitrary", "arbitrary"),
                                             vmem_limit_bytes=VMEM_LIMIT),
        name="inproj_prompt",
    )(x, mod4, mod4, g_mix, w_in, cos, sin, conv_w, conv_b, ln_g, ln_b)


def _head_rows_mask(rows):
    lane_head = lax.broadcasted_iota(jnp.int32, (rows, D_ATTN), 1) // HEAD_DIM
    return lane_head == lax.broadcasted_iota(jnp.int32, (rows, D_ATTN), 0) % N_HEADS


def _attn_prompt_kernel(q_ref, k_ref, vt_ref, km_ref, o_ref, qt_ref, sel_ref, *, nb):
    qb = pl.program_id(1)
    tq = MOBA_BLOCK
    q = q_ref[0]

    hm = _head_rows_mask(N_HEADS)
    km = km_ref[0]
    kmt = jnp.concatenate([jnp.where(hm, km[n:n + 1, :], 0.0) for n in range(nb)], axis=0)
    gate = lax.dot_general(kmt, q, NT_DIMS, precision=lax.Precision.HIGHEST, preferred_element_type=F32)
    g = [gate[n * N_HEADS:(n + 1) * N_HEADS, :] for n in range(nb)]
    for n in range(nb - 1):
        rank = jnp.zeros((N_HEADS, tq), F32)
        for m in range(nb - 1):
            if m != n:
                beats = (g[m] >= g[n]) if m < n else (g[m] > g[n])
                rank = rank + jnp.where(beats, (m < qb).astype(F32), 0.0)
        sel_ref[n] = jnp.where(rank < MOBA_TOPK, 1.0, 0.0)

    qt = (q * HEAD_DIM ** -0.5).T
    sub = lax.broadcasted_iota(jnp.int32, (LANES, tq), 0)
    for p in range(N_HEAD_PAIRS):
        pair = qt[p * LANES:(p + 1) * LANES, :]
        qt_ref[2 * p] = jnp.where(sub < HEAD_DIM, pair, 0.0).astype(BF16)
        qt_ref[2 * p + 1] = jnp.where(sub < HEAD_DIM, 0.0, pair).astype(BF16)

    def block(n, start, mask_of, state):
        ms, ls, accs = (list(s) for s in state) if state is not None else (None, None, None)
        scores = []
        for p in range(N_HEAD_PAIRS):
            kk = k_ref[0, pl.ds(start, MOBA_BLOCK), p * LANES:(p + 1) * LANES]
            for hh in range(HEADS_PER_TILE):
                scores.append(jnp.dot(kk, qt_ref[HEADS_PER_TILE * p + hh], preferred_element_type=F32))
        out_m, out_l, out_acc = [], [], []
        for h in range(N_HEADS):
            s = jnp.where(mask_of(h), scores[h], NEG_INF)
            m_blk = jnp.max(s, axis=0, keepdims=True)
            m_new = m_blk if state is None else jnp.maximum(ms[h], m_blk)
            e = jnp.exp(s - m_new)
            l_new = jnp.sum(e, axis=0, keepdims=True)
            pv = jnp.dot(vt_ref[0, n, h * HEAD_DIM:(h + 1) * HEAD_DIM, :], e.astype(BF16),
                         preferred_element_type=F32)
            if state is not None:
                alpha = jnp.exp(ms[h] - m_new)
                l_new = alpha * ls[h] + l_new
                pv = alpha * accs[h] + pv
            out_m.append(m_new)
            out_l.append(l_new)
            out_acc.append(pv)
        return tuple(out_m), tuple(out_l), tuple(out_acc)

    causal = (lax.broadcasted_iota(jnp.int32, (tq, tq), 0) <= lax.broadcasted_iota(jnp.int32, (tq, tq), 1))
    state = block(qb, pl.multiple_of(qb * MOBA_BLOCK, MOBA_BLOCK), lambda h: causal, None)

    def past_block(n, state):
        return block(n, pl.multiple_of(n * MOBA_BLOCK, MOBA_BLOCK), lambda h: sel_ref[n, h:h + 1, :] > 0.5, state)

    _, ls, accs = lax.fori_loop(0, qb, past_block, state)
    out = jnp.concatenate([accs[h] / ls[h] for h in range(N_HEADS)], axis=0)
    o_ref[0] = out.T.astype(BF16)


def _attn_prompt(q, kb, vt, kmean):
    bsz, s, _ = q.shape
    nb = s // MOBA_BLOCK
    return pl.pallas_call(
        functools.partial(_attn_prompt_kernel, nb=nb),
        out_shape=jax.ShapeDtypeStruct((bsz, s, D_ATTN), BF16),
        grid=(bsz, nb),
        in_specs=[pl.BlockSpec((1, MOBA_BLOCK, D_ATTN), lambda b, i: (b, i, 0)),
                  pl.BlockSpec((1, s, D_ATTN), lambda b, i: (b, 0, 0)),
                  pl.BlockSpec((1, nb, D_ATTN, MOBA_BLOCK), lambda b, i: (b, 0, 0, 0)),
                  pl.BlockSpec((1, nb, D_ATTN), lambda b, i: (b, 0, 0))],
        out_specs=pl.BlockSpec((1, MOBA_BLOCK, D_ATTN), lambda b, i: (b, i, 0)),
        scratch_shapes=[pltpu.VMEM((N_HEADS, LANES, MOBA_BLOCK), BF16),
                        pltpu.VMEM((max(nb - 1, 1), N_HEADS, MOBA_BLOCK), F32)],
        compiler_params=pltpu.CompilerParams(dimension_semantics=("arbitrary", "arbitrary"),
                                             vmem_limit_bytes=VMEM_LIMIT),
        name="attn_prompt",
    )(q, kb, vt, kmean)


def _mlp_kernel(x_ref, a_ref, cbn_ref, ga_ref, gb_ref, gtm_ref, shf_ref, scf_ref, gtf_ref, gffn_ref, gfin_ref,
                wao_ref, wco_ref, wout_ref, wfi_ref, wfo_ref, o_ref, *, final):
    a = jnp.dot(a_ref[...].astype(BF16), wao_ref[...], preferred_element_type=F32)
    cb = jnp.dot(cbn_ref[...], wco_ref[...], preferred_element_type=F32)
    merged = ga_ref[...].astype(F32) * a + gb_ref[...].astype(F32) * cb
    x = x_ref[...] + gtm_ref[...] * jnp.dot(merged.astype(BF16), wout_ref[...], preferred_element_type=F32)
    hb = _rms_mod(x, gffn_ref[...], scf_ref[...], shf_ref[...]).astype(BF16)
    def gate_up(c):
        return (jnp.dot(hb, wfi_ref[:, c:c + FFN_CHUNK], preferred_element_type=F32),
                jnp.dot(hb, wfi_ref[:, D_FF + c:D_FF + c + FFN_CHUNK], preferred_element_type=F32))

    acc = jnp.zeros(x.shape, F32)
    nxt = gate_up(0)
    for c in range(0, D_FF, FFN_CHUNK):
        gate, up = nxt
        if c + FFN_CHUNK < D_FF:
            nxt = gate_up(c + FFN_CHUNK)
        acc = acc + jnp.dot((_silu(gate) * up).astype(BF16), wfo_ref[c:c + FFN_CHUNK, :], preferred_element_type=F32)
    x = x + gtf_ref[...] * acc
    if final:
        x = x * lax.rsqrt(jnp.mean(x * x, axis=-1, keepdims=True) + EPS) * gfin_ref[...]
    o_ref[...] = x


def _mlp(x, attn, cbn, ga, gb, mod_rows, g_ffn, g_final, w_attn_o, w_conv_o, w_out, w_ffn_in, w_ffn_o,
         *, tm, rows_per_mod, final):
    m = x.shape[0]
    tok = lambda n: pl.BlockSpec((tm, n), lambda i: (i, 0))
    vec = lambda n: pl.BlockSpec((1, n), lambda i: (0, 0))
    if rows_per_mod == 1:
        mod = lambda j: pl.BlockSpec((tm, D_MODEL), lambda i: (i, j))
    else:
        assert rows_per_mod % tm == 0
        mod_rows = mod_rows.reshape(mod_rows.shape[0], 1, 6 * D_MODEL)
        mod = lambda j: pl.BlockSpec((None, 1, D_MODEL), lambda i: (i // (rows_per_mod // tm), 0, j))
    return pl.pallas_call(
        functools.partial(_mlp_kernel, final=final),
        out_shape=jax.ShapeDtypeStruct((m, D_MODEL), F32),
        grid=(m // tm,),
        in_specs=[tok(D_MODEL), tok(D_ATTN), tok(D_CONV), tok(D_MODEL), tok(D_MODEL),
                  mod(2), mod(3), mod(4), mod(5), vec(D_MODEL), vec(D_MODEL),
                  _resident((D_ATTN, D_MODEL)), _resident((D_CONV, D_MODEL)), _resident((D_MODEL, D_MODEL)),
                  _resident((D_MODEL, 2 * D_FF)), _resident((D_FF, D_MODEL))],
        out_specs=tok(D_MODEL),
        compiler_params=pltpu.CompilerParams(dimension_semantics=("arbitrary",), vmem_limit_bytes=VMEM_LIMIT),
        name="mixer_out_ffn",
    )(x, attn, cbn, ga, gb, mod_rows, mod_rows, mod_rows, mod_rows, g_ffn, g_final,
      w_attn_o, w_conv_o, w_out, w_ffn_in, w_ffn_o)


def _inproj_sample_kernel(x_ref, sh_ref, sc_ref, g_ref, w_ref, cos_ref, sin_ref, st_ref, cw_ref, cb_ref, lng_ref,
                          lnb_ref, q_ref, k_ref, v_ref, cbn_ref, ga_ref, gb_ref, cs_ref, uc, *, db, t_new):
    hb = _rms_mod(x_ref[...], g_ref[...], sc_ref[...], sh_ref[...]).astype(BF16)

    def proj(lo, hi):
        return jnp.dot(hb, w_ref[:, lo:hi], preferred_element_type=F32)

    cos, sin = cos_ref[...], sin_ref[...]
    q_ref[...] = _rope(proj(0, D_ATTN), cos, sin)
    k_ref[...] = _rope(proj(D_ATTN, 2 * D_ATTN), cos, sin)
    v_ref[...] = proj(2 * D_ATTN, 3 * D_ATTN)
    u0 = 3 * D_ATTN
    glu = proj(u0, u0 + D_CONV) * _sigmoid(proj(u0 + D_CONV, u0 + 2 * D_CONV))
    g0 = u0 + 2 * D_CONV
    ga_ref[...] = _sigmoid(proj(g0, g0 + D_MODEL)).astype(BF16)
    gb_ref[...] = _sigmoid(proj(g0 + D_MODEL, g0 + 2 * D_MODEL)).astype(BF16)

    ctx = CONV_K - 1
    uc[:, 0:ctx, :] = st_ref[...]
    uc[:, ctx:ctx + t_new, :] = glu.reshape(db, t_new, D_CONV)
    acc = jnp.zeros((db, t_new, D_CONV), F32) + cb_ref[...]
    for j in range(CONV_K):
        acc = acc + cw_ref[pl.ds(j, 1), :] * uc[:, pl.ds(j, t_new), :]
    cbn_ref[...] = _layer_norm_silu(acc.reshape(db * t_new, D_CONV), lng_ref[...], lnb_ref[...]).astype(BF16)
    cs_ref[...] = uc[:, pl.ds(t_new, ctx), :]


def _inproj_sample(x, sh, sc, g_mix, w_in, cos, sin, state, conv_w, conv_b, ln_g, ln_b, *, db, t_new):
    m = db * t_new
    full = lambda *shape: pl.BlockSpec(shape, lambda i: (0,) * len(shape))
    out_shape = (
        jax.ShapeDtypeStruct((m, D_ATTN), F32), jax.ShapeDtypeStruct((m, D_ATTN), F32),
        jax.ShapeDtypeStruct((m, D_ATTN), F32), jax.ShapeDtypeStruct((m, D_CONV), BF16),
        jax.ShapeDtypeStruct((m, D_MODEL), BF16), jax.ShapeDtypeStruct((m, D_MODEL), BF16),
        jax.ShapeDtypeStruct((db, CONV_K - 1, D_CONV), F32),
    )
    ctx_rows = -(-(CONV_K - 1 + t_new) // SUBLANES) * SUBLANES
    return pl.pallas_call(
        functools.partial(_inproj_sample_kernel, db=db, t_new=t_new),
        out_shape=out_shape,
        grid=(1,),
        in_specs=[full(m, D_MODEL), pl.BlockSpec((m, D_MODEL), lambda i: (0, 0)),
                  pl.BlockSpec((m, D_MODEL), lambda i: (0, 1)), full(1, D_MODEL), full(D_MODEL, D_IN),
                  full(m, LANES), full(m, LANES), full(db, CONV_K - 1, D_CONV), full(CONV_K, D_CONV),
                  full(1, D_CONV), full(1, D_CONV), full(1, D_CONV)],
        out_specs=(full(m, D_ATTN), full(m, D_ATTN), full(m, D_ATTN), full(m, D_CONV), full(m, D_MODEL),
                   full(m, D_MODEL), full(db, CONV_K - 1, D_CONV)),
        scratch_shapes=[pltpu.VMEM((db, ctx_rows, D_CONV), F32)],
        compiler_params=pltpu.CompilerParams(dimension_semantics=("arbitrary",), vmem_limit_bytes=VMEM_LIMIT),
        name="inproj_sample",
    )(x, sh, sc, g_mix, w_in, cos, sin, state, conv_w, conv_b, ln_g, ln_b)


def _select_kernel(pt_ref, q_ref, ck_ref, idx_ref, buf, km, sems, *, layer, n_pages, chunk, t_new, db):
    b = pl.program_id(0)
    n_chunks = n_pages // chunk

    def copies(seq, c, slot):
        return [pltpu.make_async_copy(ck_ref.at[layer, pt_ref[seq * n_pages + c * chunk + g]], buf.at[slot, g],
                                      sems.at[slot]) for g in range(chunk)]

    @pl.when(b == 0)
    def _():
        for cp in copies(0, 0, 0):
            cp.start()

    for c in range(n_chunks):
        slot = c % 2
        if c + 1 < n_chunks:
            for cp in copies(b, c + 1, 1 - slot):
                cp.start()
        else:
            @pl.when(b + 1 < db)
            def _():
                for cp in copies(b + 1, 0, 1 - slot):
                    cp.start()
        for cp in copies(b, c, slot):
            cp.wait()
        for j in range(chunk // PAGES_PER_BLOCK):
            tot = jnp.zeros((N_HEADS, HEAD_DIM), F32)
            for g in range(PAGES_PER_BLOCK):
                tot = tot + jnp.sum(buf[slot, j * PAGES_PER_BLOCK + g], axis=0)
            km[c * (chunk // PAGES_PER_BLOCK) + j] = tot * (1.0 / MOBA_BLOCK)

    n_blk = n_pages // PAGES_PER_BLOCK
    kmean = km[...]
    blk = lax.broadcasted_iota(jnp.int32, (n_blk, N_HEADS), 0)
    rows = []
    for t in range(t_new):
        gate = jnp.sum(kmean * q_ref[0, t][None], axis=-1)
        for _ in range(MOBA_TOPK):
            best = jnp.max(gate, axis=0, keepdims=True)
            idx = jnp.min(jnp.where(gate == best, blk, n_blk), axis=0, keepdims=True)
            rows.append(idx)
            gate = jnp.where(blk == idx, NEG_INF, gate)
    idx_ref[0] = jnp.concatenate(rows, axis=0)


def _select_blocks(page_table, q, cache_k, *, layer):
    db, t_new = q.shape[:2]
    n_pages = page_table.shape[1]
    n_blk = n_pages // PAGES_PER_BLOCK
    chunk = 16
    assert n_pages % (2 * chunk) == 0 and chunk % PAGES_PER_BLOCK == 0
    grid_spec = pltpu.PrefetchScalarGridSpec(
        num_scalar_prefetch=1,
        grid=(db,),
        in_specs=[pl.BlockSpec((1, t_new, N_HEADS, HEAD_DIM), lambda b, pt: (b, 0, 0, 0)),
                  pl.BlockSpec(memory_space=pl.ANY)],
        out_specs=pl.BlockSpec((1, t_new * MOBA_TOPK, N_HEADS), lambda b, pt: (b, 0, 0)),
        scratch_shapes=[pltpu.VMEM((2, chunk, PAGE_SIZE, N_HEADS, HEAD_DIM), F32),
                        pltpu.VMEM((n_blk, N_HEADS, HEAD_DIM), F32),
                        pltpu.SemaphoreType.DMA((2,))],
    )
    return pl.pallas_call(
        functools.partial(_select_kernel, layer=layer, n_pages=n_pages, chunk=chunk, t_new=t_new, db=db),
        out_shape=jax.ShapeDtypeStruct((db, t_new * MOBA_TOPK, N_HEADS), jnp.int32),
        grid_spec=grid_spec,
        compiler_params=pltpu.CompilerParams(dimension_semantics=("arbitrary",), vmem_limit_bytes=VMEM_LIMIT),
        name="select_blocks",
    )(page_table.reshape(-1), q, cache_k)


def _attn_sample_kernel(sel_ref, pt_ref, q_ref, kn_ref, vn_ref, ck_ref, cv_ref, o_ref, kbuf, vbuf, sems,
                        *, layer, n_pages, t_new, db):
    b = pl.program_id(0)
    n_slots = t_new * MOBA_TOPK
    scale = HEAD_DIM ** -0.5

    def copies(seq, h, par, slot):
        blk = sel_ref[(seq * n_slots + slot) * N_HEADS + h]
        out = []
        for g in range(PAGES_PER_BLOCK):
            page = pt_ref[seq * n_pages + blk * PAGES_PER_BLOCK + g]
            rows = pl.ds(g * PAGE_SIZE, PAGE_SIZE)
            out.append(pltpu.make_async_copy(ck_ref.at[layer, page, :, h, :], kbuf.at[par, slot, rows, :], sems.at[0, par]))
            out.append(pltpu.make_async_copy(cv_ref.at[layer, page, :, h, :], vbuf.at[par, slot, rows, :], sems.at[1, par]))
        return out

    def start_head(seq, h, par):
        def body(slot, carry):
            for cp in copies(seq, h, par, slot):
                cp.start()
            return carry
        lax.fori_loop(0, n_slots, body, 0)

    def wait_head(seq, h, par):
        def body(slot, carry):
            for cp in copies(seq, h, par, slot):
                cp.wait()
            return carry
        lax.fori_loop(0, n_slots, body, 0)

    @pl.when(b == 0)
    def _():
        start_head(0, 0, 0)

    row = lax.broadcasted_iota(jnp.int32, (t_new, MOBA_BLOCK), 0)
    tok = lax.broadcasted_iota(jnp.int32, (t_new, 1), 0)
    for h in range(N_HEADS):
        par = h % 2
        if h + 1 < N_HEADS:
            start_head(b, h + 1, 1 - par)
        else:
            @pl.when(b + 1 < db)
            def _():
                start_head(b + 1, 0, 1 - par)
        wait_head(b, h, par)

        q = q_ref[0, h] * scale
        qb16 = q.astype(BF16)

        s_all = [lax.dot_general(qb16, kbuf[par, slot].astype(BF16), NT_DIMS, preferred_element_type=F32)
                 for slot in range(n_slots)]
        e = []
        for r in range(MOBA_TOPK):
            er = s_all[r]
            for t in range(1, t_new):
                er = jnp.where(row == t, s_all[t * MOBA_TOPK + r], er)
            e.append(er)
        kn, vn = kn_ref[0, h], vn_ref[0, h]
        s_new = [jnp.where(tok >= j, jnp.sum(q * kn[j:j + 1, :], axis=-1, keepdims=True), NEG_INF)
                 for j in range(t_new)]
        m = functools.reduce(jnp.maximum, s_new + [jnp.max(er, axis=-1, keepdims=True) for er in e])
        p = [jnp.exp(er - m) for er in e]
        p_new = [jnp.exp(s - m) for s in s_new]
        den = functools.reduce(jnp.add, p_new + [jnp.sum(pr, axis=-1, keepdims=True) for pr in p])
        num = functools.reduce(jnp.add, [p_new[j] * vn[j:j + 1, :] for j in range(t_new)])

        parts = [jnp.dot(jnp.where(row == t, p[r], 0.0).astype(BF16), vbuf[par, t * MOBA_TOPK + r].astype(BF16),
                         preferred_element_type=F32) for t in range(t_new) for r in range(MOBA_TOPK)]
        o_ref[0, h] = (num + functools.reduce(jnp.add, parts)) / den


def _attn_sample(sel, page_table, q, k_new, v_new, cache_k, cache_v, *, layer):
    db, _, t_new, _ = q.shape
    n_pages = page_table.shape[1]
    n_slots = t_new * MOBA_TOPK
    tok = pl.BlockSpec((1, N_HEADS, t_new, HEAD_DIM), lambda b, s, pt: (b, 0, 0, 0))
    grid_spec = pltpu.PrefetchScalarGridSpec(
        num_scalar_prefetch=2,
        grid=(db,),
        in_specs=[tok, tok, tok, pl.BlockSpec(memory_space=pl.ANY), pl.BlockSpec(memory_space=pl.ANY)],
        out_specs=tok,
        scratch_shapes=[pltpu.VMEM((2, n_slots, MOBA_BLOCK, HEAD_DIM), F32),
                        pltpu.VMEM((2, n_slots, MOBA_BLOCK, HEAD_DIM), F32),
                        pltpu.SemaphoreType.DMA((2, 2))],
    )
    return pl.pallas_call(
        functools.partial(_attn_sample_kernel, layer=layer, n_pages=n_pages, t_new=t_new, db=db),
        out_shape=jax.ShapeDtypeStruct((db, N_HEADS, t_new, HEAD_DIM), F32),
        grid_spec=grid_spec,
        compiler_params=pltpu.CompilerParams(dimension_semantics=("arbitrary",), vmem_limit_bytes=VMEM_LIMIT),
        name="attn_sample",
    )(sel.reshape(-1), page_table.reshape(-1), q, k_new, v_new, cache_k, cache_v)


def _rope_tables(pos):
    half = HEAD_DIM // 2
    inv_freq = ROPE_THETA ** (-jnp.arange(half, dtype=F32) / half)
    ang = pos.astype(F32)[:, None] * inv_freq[None, :]
    cos, sin = jnp.cos(ang), jnp.sin(ang)
    cos = jnp.concatenate([cos, cos] * HEADS_PER_TILE, axis=-1)
    sin = jnp.concatenate([-sin, sin] * HEADS_PER_TILE, axis=-1)
    return cos, sin


def kernel(x_prompt, x_sample, cache_k, cache_v, state_conv, page_table, c_prompt, c_sample, w_ada, b_ada, g_mix, w_in, w_attn_o, conv_w, conv_b, conv_norm_g, conv_norm_b, w_conv_o, w_out, g_ffn, w_ffn_in, w_ffn_o, g_final):
    bsz, seq, _ = x_prompt.shape
    db, t_new, _ = x_sample.shape
    depth = w_in.shape[0]
    n_pages = page_table.shape[1]
    past = n_pages * PAGE_SIZE
    assert seq % MOBA_BLOCK == 0 and past % MOBA_BLOCK == 0 and t_new <= MOBA_BLOCK and t_new % SUBLANES == 0
    assert past // MOBA_BLOCK >= MOBA_TOPK
    assert cache_k.shape[2:] == (PAGE_SIZE, N_HEADS, HEAD_DIM)
    tm = 512 if seq % 512 == 0 else MOBA_BLOCK

    cos_p, sin_p = _rope_tables(jnp.arange(seq, dtype=jnp.int32))
    cos_s, sin_s = _rope_tables(past + jnp.arange(t_new, dtype=jnp.int32))
    cos_s, sin_s = jnp.tile(cos_s, (db, 1)), jnp.tile(sin_s, (db, 1))

    mod = _ada(jnp.concatenate([c_prompt, c_sample], axis=0), w_ada, b_ada)
    mod4 = mod.reshape(depth, bsz + db, 1, 6 * D_MODEL)
    w_in_b, w_attn_o_b, w_conv_o_b = w_in.astype(BF16), w_attn_o.astype(BF16), w_conv_o.astype(BF16)
    w_out_b, w_ffn_in_b, w_ffn_o_b = w_out.astype(BF16), w_ffn_in.astype(BF16), w_ffn_o.astype(BF16)
    row2 = lambda a, l: a[l][None, :]
    g_fin = g_final[None, :]
    by_token = lambda a: a.reshape(db, t_new, N_HEADS, HEAD_DIM)
    by_head = lambda a: by_token(a).transpose(0, 2, 1, 3)

    xp = x_prompt
    xs = x_sample.reshape(db * t_new, D_MODEL)
    kp, vp, cp, ksm, vsm, csm = [], [], [], [], [], []
    for l in range(depth):
        final = l == depth - 1
        lw = (row2(g_mix, l), w_in_b[l])
        cw = (conv_w[l], row2(conv_b, l), row2(conv_norm_g, l), row2(conv_norm_b, l))
        mw = (row2(g_ffn, l), g_fin, w_attn_o_b[l], w_conv_o_b[l], w_out_b[l], w_ffn_in_b[l], w_ffn_o_b[l])

        q, k, v, kb, vt, kmean, cbn, ga, gb, cs = _inproj_prompt(xp, mod4, l, *lw, cos_p, sin_p, *cw, tm=tm)
        attn = _attn_prompt(q, kb, vt, kmean.reshape(bsz, seq // MOBA_BLOCK, D_ATTN))
        flat = lambda a: a.reshape(bsz * seq, a.shape[-1])
        xp = _mlp(flat(xp), flat(attn), flat(cbn), flat(ga), flat(gb), mod[l, :bsz], *mw,
                  tm=tm, rows_per_mod=seq, final=final).reshape(bsz, seq, D_MODEL)
        kp.append(k.reshape(bsz, seq, N_HEADS, HEAD_DIM))
        vp.append(v.reshape(bsz, seq, N_HEADS, HEAD_DIM))
        cp.append(cs)

        mod_s = jnp.repeat(mod[l, bsz:], t_new, axis=0)
        qs, ks, vs, cbn_s, ga_s, gb_s, cs_s = _inproj_sample(xs, mod_s, mod_s, *lw, cos_s, sin_s, state_conv[l], *cw,
                                                             db=db, t_new=t_new)
        sel = _select_blocks(page_table, by_token(qs), cache_k, layer=l)
        attn_s = _attn_sample(sel, page_table, by_head(qs), by_head(ks), by_head(vs), cache_k, cache_v, layer=l)
        attn_s = attn_s.transpose(0, 2, 1, 3).reshape(db * t_new, D_ATTN)
        xs = _mlp(xs, attn_s, cbn_s, ga_s, gb_s, mod_s, *mw, tm=db * t_new, rows_per_mod=1, final=final)
        ksm.append(by_token(ks))
        vsm.append(by_token(vs))
        csm.append(cs_s)

    return (xp, xs.reshape(db, t_new, D_MODEL), jnp.stack(kp), jnp.stack(vp), jnp.stack(cp),
            jnp.stack(ksm), jnp.stack(vsm), jnp.stack(csm))
```

```python
import functools

import jax
import jax.numpy as jnp
from jax import lax
from jax.experimental import pallas as pl
from jax.experimental.pallas import tpu as pltpu

D_MODEL = 1024
N_HEADS = 8
HEAD_DIM = 64
D_ATTN = N_HEADS * HEAD_DIM
D_CONV = D_MODEL // 2
CONV_K = 31
CONV_HALO = 32
MOBA_BLOCK = 256
MOBA_TOPK = 3
PAGE_SIZE = 128
PAGES_PER_BLOCK = MOBA_BLOCK // PAGE_SIZE
ROPE_THETA = 10000.0
D_FF = 2816
EPS = 1e-6
D_IN = 3 * D_ATTN + 2 * D_CONV + 2 * D_MODEL
LANES = 128
SUBLANES = 8
HEADS_PER_TILE = LANES // HEAD_DIM
N_HEAD_PAIRS = N_HEADS // HEADS_PER_TILE
FFN_CHUNK = 256
VMEM_LIMIT = 56 * 1024 * 1024

F32 = jnp.float32
BF16 = jnp.bfloat16
NEG_INF = float("-inf")
NT_DIMS = (((1,), (1,)), ((), ()))


def _resident(shape):
    return pl.BlockSpec(shape, lambda *_: (0,) * len(shape), pipeline_mode=pl.Buffered(1))


def _sigmoid(x):
    return 0.5 * jnp.tanh(0.5 * x) + 0.5


def _silu(x):
    return x * _sigmoid(x)


def _rms_mod(x, g, sc, sh):
    y = x * lax.rsqrt(jnp.mean(x * x, axis=-1, keepdims=True) + EPS) * g
    return y * (1.0 + sc) + sh


def _rope(y, cos, sin):
    reps = D_ATTN // LANES
    cos = jnp.concatenate([cos] * reps, axis=-1)
    sin = jnp.concatenate([sin] * reps, axis=-1)
    half = HEAD_DIM // 2
    lane = lax.broadcasted_iota(jnp.int32, y.shape, 1)
    first = (lane % HEAD_DIM) < half
    partner = jnp.where(first, pltpu.roll(y, D_ATTN - half, 1), pltpu.roll(y, half, 1))
    return y * cos + partner * sin


def _layer_norm_silu(c, g, b):
    mu = jnp.mean(c, axis=-1, keepdims=True)
    d = c - mu
    y = d * lax.rsqrt(jnp.mean(d * d, axis=-1, keepdims=True) + EPS) * g + b
    return _silu(y)


def _ada_kernel(c_ref, w_ref, b_ref, o_ref):
    c = c_ref[...]
    o_ref[0] = jnp.dot(_silu(c).astype(BF16), w_ref[0].astype(BF16), preferred_element_type=F32) + b_ref[0]


def _ada(c_all, w_ada, b_ada):
    depth, _, n_out = w_ada.shape
    rows = c_all.shape[0]
    tn = 1024
    return pl.pallas_call(
        _ada_kernel,
        out_shape=jax.ShapeDtypeStruct((depth, rows, n_out), F32),
        grid=(depth, n_out // tn),
        in_specs=[pl.BlockSpec((rows, D_MODEL), lambda l, j: (0, 0)),
                  pl.BlockSpec((1, D_MODEL, tn), lambda l, j: (l, 0, j)),
                  pl.BlockSpec((1, 1, tn), lambda l, j: (l, 0, j))],
        out_specs=pl.BlockSpec((1, rows, tn), lambda l, j: (l, 0, j)),
        compiler_params=pltpu.CompilerParams(dimension_semantics=("arbitrary", "arbitrary"),
                                             vmem_limit_bytes=VMEM_LIMIT),
        name="ada_mod",
    )(c_all, w_ada, b_ada.reshape(depth, 1, n_out))


def _inproj_prompt_kernel(x_ref, sh_ref, sc_ref, g_ref, w_ref, cos_ref, sin_ref, cw_ref, cb_ref, lng_ref, lnb_ref,
                          q_ref, k_ref, v_ref, kb_ref, vt_ref, km_ref, cbn_ref, ga_ref, gb_ref, cs_ref,
                          ubuf, zbuf, *, tm, nt):
    t = pl.program_id(1)
    hb = _rms_mod(x_ref[0], g_ref[...], sc_ref[0, 0], sh_ref[0, 0]).astype(BF16)

    def proj(lo, hi):
        return jnp.dot(hb, w_ref[:, lo:hi], preferred_element_type=F32)

    cos, sin = cos_ref[...], sin_ref[...]
    q_ref[0] = _rope(proj(0, D_ATTN), cos, sin)
    k = _rope(proj(D_ATTN, 2 * D_ATTN), cos, sin)
    k_ref[0] = k
    kb_ref[0] = k.astype(BF16)
    for j in range(tm // MOBA_BLOCK):
        km_ref[0, j] = jnp.sum(k[j * MOBA_BLOCK:(j + 1) * MOBA_BLOCK], axis=0, keepdims=True) * (1.0 / MOBA_BLOCK)
    v = proj(2 * D_ATTN, 3 * D_ATTN)
    v_ref[0] = v
    for j in range(tm // MOBA_BLOCK):
        vt_ref[0, j] = v[j * MOBA_BLOCK:(j + 1) * MOBA_BLOCK].T.astype(BF16)
    u0 = 3 * D_ATTN
    glu = proj(u0, u0 + D_CONV) * _sigmoid(proj(u0 + D_CONV, u0 + 2 * D_CONV))
    g0 = u0 + 2 * D_CONV
    ga_ref[0] = _sigmoid(proj(g0, g0 + D_MODEL)).astype(BF16)
    gb_ref[0] = _sigmoid(proj(g0 + D_MODEL, g0 + 2 * D_MODEL)).astype(BF16)

    @pl.when(t == 0)
    def _():
        ubuf[0:CONV_HALO, :] = jnp.zeros((CONV_HALO, D_CONV), F32)

    ubuf[CONV_HALO:CONV_HALO + tm, :] = glu
    acc = jnp.zeros((tm, D_CONV), F32) + cb_ref[...]
    for r in range(SUBLANES):
        z = None
        for a in range(-(-CONV_K // SUBLANES)):
            d = SUBLANES * a + r
            if d < CONV_K:
                term = (cw_ref[pl.ds(CONV_K - 1 - d, 1), :]
                        * ubuf[pl.ds(CONV_HALO - SUBLANES * (a + 1), tm + SUBLANES), :])
                z = term if z is None else z + term
        zbuf[...] = z
        acc = acc + zbuf[pl.ds(SUBLANES - r, tm), :]
    cbn_ref[0] = _layer_norm_silu(acc, lng_ref[...], lnb_ref[...]).astype(BF16)

    @pl.when(t == nt - 1)
    def _():
        cs_ref[0] = ubuf[pl.ds(CONV_HALO + tm - (CONV_K - 1), CONV_K - 1), :]

    ubuf[0:CONV_HALO, :] = ubuf[tm:tm + CONV_HALO, :]


def _inproj_prompt(x, mod4, l, g_mix, w_in, cos, sin, conv_w, conv_b, ln_g, ln_b, *, tm):
    bsz, s, _ = x.shape
    nt = s // tm
    nb = s // MOBA_BLOCK
    row = lambda b, t: (b, t, 0)
    vec = lambda n: pl.BlockSpec((1, n), lambda b, t: (0, 0))
    mod = lambda j: pl.BlockSpec((1, 1, 1, D_MODEL), lambda b, t: (l, b, 0, j))
    tok = lambda n: pl.BlockSpec((1, tm, n), row)
    out_shape = (
        jax.ShapeDtypeStruct((bsz, s, D_ATTN), F32),
        jax.ShapeDtypeStruct((bsz, s, D_ATTN), F32),
        jax.ShapeDtypeStruct((bsz, s, D_ATTN), F32),
        jax.ShapeDtypeStruct((bsz, s, D_ATTN), BF16),
        jax.ShapeDtypeStruct((bsz, nb, D_ATTN, MOBA_BLOCK), BF16),
        jax.ShapeDtypeStruct((bsz, nb, 1, D_ATTN), F32),
        jax.ShapeDtypeStruct((bsz, s, D_CONV), BF16),
        jax.ShapeDtypeStruct((bsz, s, D_MODEL), BF16),
        jax.ShapeDtypeStruct((bsz, s, D_MODEL), BF16),
        jax.ShapeDtypeStruct((bsz, CONV_K - 1, D_CONV), F32),
    )
    out_specs = (tok(D_ATTN), tok(D_ATTN), tok(D_ATTN), tok(D_ATTN),
                 pl.BlockSpec((1, tm // MOBA_BLOCK, D_ATTN, MOBA_BLOCK), lambda b, t: (b, t, 0, 0)),
                 pl.BlockSpec((1, tm // MOBA_BLOCK, 1, D_ATTN), lambda b, t: (b, t, 0, 0)),
                 tok(D_CONV), tok(D_MODEL), tok(D_MODEL),
                 pl.BlockSpec((1, CONV_K - 1, D_CONV), lambda b, t: (b, 0, 0)))
    return pl.pallas_call(
        functools.partial(_inproj_prompt_kernel, tm=tm, nt=nt),
        out_shape=out_shape,
        grid=(bsz, nt),
        in_specs=[tok(D_MODEL), mod(0), mod(1), vec(D_MODEL), _resident((D_MODEL, D_IN)),
                  pl.BlockSpec((tm, LANES), lambda b, t: (t, 0)), pl.BlockSpec((tm, LANES), lambda b, t: (t, 0)),
                  pl.BlockSpec((CONV_K, D_CONV), lambda b, t: (0, 0)), vec(D_CONV), vec(D_CONV), vec(D_CONV)],
        out_specs=out_specs,
        scratch_shapes=[pltpu.VMEM((CONV_HALO + tm, D_CONV), F32), pltpu.VMEM((tm + SUBLANES, D_CONV), F32)],
        compiler_params=pltpu.CompilerParams(dimension_semantics=("arbitrary", "arbitrary"),
                                             vmem_limit_bytes=VMEM_LIMIT),
        name="inproj_prompt",
    )(x, mod4, mod4, g_mix, w_in, cos, sin, conv_w, conv_b, ln_g, ln_b)


def _head_rows_mask(rows):
    lane_head = lax.broadcasted_iota(jnp.int32, (rows, D_ATTN), 1) // HEAD_DIM
    return lane_head == lax.broadcasted_iota(jnp.int32, (rows, D_ATTN), 0) % N_HEADS


def _attn_prompt_kernel(q_ref, k_ref, vt_ref, km_ref, o_ref, qt_ref, sel_ref, *, nb):
    qb = pl.program_id(1)
    tq = MOBA_BLOCK
    q = q_ref[0]

    hm = _head_rows_mask(N_HEADS)
    km = km_ref[0]
    kmt = jnp.concatenate([jnp.where(hm, km[n:n + 1, :], 0.0) for n in range(nb)], axis=0)
    gate = lax.dot_general(kmt, q, NT_DIMS, precision=lax.Precision.HIGHEST, preferred_element_type=F32)
    g = [gate[n * N_HEADS:(n + 1) * N_HEADS, :] for n in range(nb)]
    for n in range(nb - 1):
        rank = jnp.zeros((N_HEADS, tq), F32)
        for m in range(nb - 1):
            if m != n:
                beats = (g[m] >= g[n]) if m < n else (g[m] > g[n])
                rank = rank + jnp.where(beats, (m < qb).astype(F32), 0.0)
        sel_ref[n] = jnp.where(rank < MOBA_TOPK, 1.0, 0.0)

    qt = (q * HEAD_DIM ** -0.5).T
    sub = lax.broadcasted_iota(jnp.int32, (LANES, tq), 0)
    for p in range(N_HEAD_PAIRS):
        pair = qt[p * LANES:(p + 1) * LANES, :]
        qt_ref[2 * p] = jnp.where(sub < HEAD_DIM, pair, 0.0).astype(BF16)
        qt_ref[2 * p + 1] = jnp.where(sub < HEAD_DIM, 0.0, pair).astype(BF16)

    def block(n, start, mask_of, state):
        ms, ls, accs = (list(s) for s in state) if state is not None else (None, None, None)
        scores = []
        for p in range(N_HEAD_PAIRS):
            kk = k_ref[0, pl.ds(start, MOBA_BLOCK), p * LANES:(p + 1) * LANES]
            for hh in range(HEADS_PER_TILE):
                scores.append(jnp.dot(kk, qt_ref[HEADS_PER_TILE * p + hh], preferred_element_type=F32))
        out_m, out_l, out_acc = [], [], []
        for h in range(N_HEADS):
            s = jnp.where(mask_of(h), scores[h], NEG_INF)
            m_blk = jnp.max(s, axis=0, keepdims=True)
            m_new = m_blk if state is None else jnp.maximum(ms[h], m_blk)
            e = jnp.exp(s - m_new)
            l_new = jnp.sum(e, axis=0, keepdims=True)
            pv = jnp.dot(vt_ref[0, n, h * HEAD_DIM:(h + 1) * HEAD_DIM, :], e.astype(BF16),
                         preferred_element_type=F32)
            if state is not None:
                alpha = jnp.exp(ms[h] - m_new)
                l_new = alpha * ls[h] + l_new
                pv = alpha * accs[h] + pv
            out_m.append(m_new)
            out_l.append(l_new)
            out_acc.append(pv)
        return tuple(out_m), tuple(out_l), tuple(out_acc)

    causal = (lax.broadcasted_iota(jnp.int32, (tq, tq), 0) <= lax.broadcasted_iota(jnp.int32, (tq, tq), 1))
    state = block(qb, pl.multiple_of(qb * MOBA_BLOCK, MOBA_BLOCK), lambda h: causal, None)

    def past_block(n, state):
        return block(n, pl.multiple_of(n * MOBA_BLOCK, MOBA_BLOCK), lambda h: sel_ref[n, h:h + 1, :] > 0.5, state)

    _, ls, accs = lax.fori_loop(0, qb, past_block, state)
    out = jnp.concatenate([accs[h] / ls[h] for h in range(N_HEADS)], axis=0)
    o_ref[0] = out.T.astype(BF16)


def _attn_prompt(q, kb, vt, kmean):
    bsz, s, _ = q.shape
    nb = s // MOBA_BLOCK
    return pl.pallas_call(
        functools.partial(_attn_prompt_kernel, nb=nb),
        out_shape=jax.ShapeDtypeStruct((bsz, s, D_ATTN), BF16),
        grid=(bsz, nb),
        in_specs=[pl.BlockSpec((1, MOBA_BLOCK, D_ATTN), lambda b, i: (b, i, 0)),
                  pl.BlockSpec((1, s, D_ATTN), lambda b, i: (b, 0, 0)),
                  pl.BlockSpec((1, nb, D_ATTN, MOBA_BLOCK), lambda b, i: (b, 0, 0, 0)),
                  pl.BlockSpec((1, nb, D_ATTN), lambda b, i: (b, 0, 0))],
        out_specs=pl.BlockSpec((1, MOBA_BLOCK, D_ATTN), lambda b, i: (b, i, 0)),
        scratch_shapes=[pltpu.VMEM((N_HEADS, LANES, MOBA_BLOCK), BF16),
                        pltpu.VMEM((max(nb - 1, 1), N_HEADS, MOBA_BLOCK), F32)],
        compiler_params=pltpu.CompilerParams(dimension_semantics=("arbitrary", "arbitrary"),
                                             vmem_limit_bytes=VMEM_LIMIT),
        name="attn_prompt",
    )(q, kb, vt, kmean)


def _mlp_kernel(x_ref, a_ref, cbn_ref, ga_ref, gb_ref, gtm_ref, shf_ref, scf_ref, gtf_ref, gffn_ref, gfin_ref,
                wao_ref, wco_ref, wout_ref, wfi_ref, wfo_ref, o_ref, *, final):
    a = jnp.dot(a_ref[...].astype(BF16), wao_ref[...], preferred_element_type=F32)
    cb = jnp.dot(cbn_ref[...], wco_ref[...], preferred_element_type=F32)
    merged = ga_ref[...].astype(F32) * a + gb_ref[...].astype(F32) * cb
    x = x_ref[...] + gtm_ref[...] * jnp.dot(merged.astype(BF16), wout_ref[...], preferred_element_type=F32)
    hb = _rms_mod(x, gffn_ref[...], scf_ref[...], shf_ref[...]).astype(BF16)
    def gate_up(c):
        return (jnp.dot(hb, wfi_ref[:, c:c + FFN_CHUNK], preferred_element_type=F32),
                jnp.dot(hb, wfi_ref[:, D_FF + c:D_FF + c + FFN_CHUNK], preferred_element_type=F32))

    acc = jnp.zeros(x.shape, F32)
    nxt = gate_up(0)
    for c in range(0, D_FF, FFN_CHUNK):
        gate, up = nxt
        if c + FFN_CHUNK < D_FF:
            nxt = gate_up(c + FFN_CHUNK)
        acc = acc + jnp.dot((_silu(gate) * up).astype(BF16), wfo_ref[c:c + FFN_CHUNK, :], preferred_element_type=F32)
    x = x + gtf_ref[...] * acc
    if final:
        x = x * lax.rsqrt(jnp.mean(x * x, axis=-1, keepdims=True) + EPS) * gfin_ref[...]
    o_ref[...] = x


def _mlp(x, attn, cbn, ga, gb, mod_rows, g_ffn, g_final, w_attn_o, w_conv_o, w_out, w_ffn_in, w_ffn_o,
         *, tm, rows_per_mod, final):
    m = x.shape[0]
    tok = lambda n: pl.BlockSpec((tm, n), lambda i: (i, 0))
    vec = lambda n: pl.BlockSpec((1, n), lambda i: (0, 0))
    if rows_per_mod == 1:
        mod = lambda j: pl.BlockSpec((tm, D_MODEL), lambda i: (i, j))
    else:
        assert rows_per_mod % tm == 0
        mod_rows = mod_rows.reshape(mod_rows.shape[0], 1, 6 * D_MODEL)
        mod = lambda j: pl.BlockSpec((None, 1, D_MODEL), lambda i: (i // (rows_per_mod // tm), 0, j))
    return pl.pallas_call(
        functools.partial(_mlp_kernel, final=final),
        out_shape=jax.ShapeDtypeStruct((m, D_MODEL), F32),
        grid=(m // tm,),
        in_specs=[tok(D_MODEL), tok(D_ATTN), tok(D_CONV), tok(D_MODEL), tok(D_MODEL),
                  mod(2), mod(3), mod(4), mod(5), vec(D_MODEL), vec(D_MODEL),
                  _resident((D_ATTN, D_MODEL)), _resident((D_CONV, D_MODEL)), _resident((D_MODEL, D_MODEL)),
                  _resident((D_MODEL, 2 * D_FF)), _resident((D_FF, D_MODEL))],
        out_specs=tok(D_MODEL),
        compiler_params=pltpu.CompilerParams(dimension_semantics=("arbitrary",), vmem_limit_bytes=VMEM_LIMIT),
        name="mixer_out_ffn",
    )(x, attn, cbn, ga, gb, mod_rows, mod_rows, mod_rows, mod_rows, g_ffn, g_final,
      w_attn_o, w_conv_o, w_out, w_ffn_in, w_ffn_o)


def _inproj_sample_kernel(x_ref, sh_ref, sc_ref, g_ref, w_ref, cos_ref, sin_ref, st_ref, cw_ref, cb_ref, lng_ref,
                          lnb_ref, q_ref, k_ref, v_ref, cbn_ref, ga_ref, gb_ref, cs_ref, uc, *, db, t_new):
    hb = _rms_mod(x_ref[...], g_ref[...], sc_ref[...], sh_ref[...]).astype(BF16)

    def proj(lo, hi):
        return jnp.dot(hb, w_ref[:, lo:hi], preferred_element_type=F32)

    cos, sin = cos_ref[...], sin_ref[...]
    q_ref[...] = _rope(proj(0, D_ATTN), cos, sin)
    k_ref[...] = _rope(proj(D_ATTN, 2 * D_ATTN), cos, sin)
    v_ref[...] = proj(2 * D_ATTN, 3 * D_ATTN)
    u0 = 3 * D_ATTN
    glu = proj(u0, u0 + D_CONV) * _sigmoid(proj(u0 + D_CONV, u0 + 2 * D_CONV))
    g0 = u0 + 2 * D_CONV
    ga_ref[...] = _sigmoid(proj(g0, g0 + D_MODEL)).astype(BF16)
    gb_ref[...] = _sigmoid(proj(g0 + D_MODEL, g0 + 2 * D_MODEL)).astype(BF16)

    ctx = CONV_K - 1
    uc[:, 0:ctx, :] = st_ref[...]
    uc[:, ctx:ctx + t_new, :] = glu.reshape(db, t_new, D_CONV)
    acc = jnp.zeros((db, t_new, D_CONV), F32) + cb_ref[...]
    for j in range(CONV_K):
        acc = acc + cw_ref[pl.ds(j, 1), :] * uc[:, pl.ds(j, t_new), :]
    cbn_ref[...] = _layer_norm_silu(acc.reshape(db * t_new, D_CONV), lng_ref[...], lnb_ref[...]).astype(BF16)
    cs_ref[...] = uc[:, pl.ds(t_new, ctx), :]


def _inproj_sample(x, sh, sc, g_mix, w_in, cos, sin, state, conv_w, conv_b, ln_g, ln_b, *, db, t_new):
    m = db * t_new
    full = lambda *shape: pl.BlockSpec(shape, lambda i: (0,) * len(shape))
    out_shape = (
        jax.ShapeDtypeStruct((m, D_ATTN), F32), jax.ShapeDtypeStruct((m, D_ATTN), F32),
        jax.ShapeDtypeStruct((m, D_ATTN), F32), jax.ShapeDtypeStruct((m, D_CONV), BF16),
        jax.ShapeDtypeStruct((m, D_MODEL), BF16), jax.ShapeDtypeStruct((m, D_MODEL), BF16),
        jax.ShapeDtypeStruct((db, CONV_K - 1, D_CONV), F32),
    )
    ctx_rows = -(-(CONV_K - 1 + t_new) // SUBLANES) * SUBLANES
    return pl.pallas_call(
        functools.partial(_inproj_sample_kernel, db=db, t_new=t_new),
        out_shape=out_shape,
        grid=(1,),
        in_specs=[full(m, D_MODEL), pl.BlockSpec((m, D_MODEL), lambda i: (0, 0)),
                  pl.BlockSpec((m, D_MODEL), lambda i: (0, 1)), full(1, D_MODEL), full(D_MODEL, D_IN),
                  full(m, LANES), full(m, LANES), full(db, CONV_K - 1, D_CONV), full(CONV_K, D_CONV),
                  full(1, D_CONV), full(1, D_CONV), full(1, D_CONV)],
        out_specs=(full(m, D_ATTN), full(m, D_ATTN), full(m, D_ATTN), full(m, D_CONV), full(m, D_MODEL),
                   full(m, D_MODEL), full(db, CONV_K - 1, D_CONV)),
        scratch_shapes=[pltpu.VMEM((db, ctx_rows, D_CONV), F32)],
        compiler_params=pltpu.CompilerParams(dimension_semantics=("arbitrary",), vmem_limit_bytes=VMEM_LIMIT),
        name="inproj_sample",
    )(x, sh, sc, g_mix, w_in, cos, sin, state, conv_w, conv_b, ln_g, ln_b)


def _select_kernel(pt_ref, q_ref, ck_ref, idx_ref, buf, kmt, sems, *, layer, n_pages, chunk, t_new, db):
    b = pl.program_id(0)
    n_chunks = n_pages // chunk

    def copies(seq, c, slot):
        return [pltpu.make_async_copy(ck_ref.at[layer, pt_ref[seq * n_pages + c * chunk + g]], buf.at[slot, g],
                                      sems.at[slot]) for g in range(chunk)]

    @pl.when(b == 0)
    def _():
        for cp in copies(0, 0, 0):
            cp.start()

    for c in range(n_chunks):
        slot = c % 2
        if c + 1 < n_chunks:
            for cp in copies(b, c + 1, 1 - slot):
                cp.start()
        else:
            @pl.when(b + 1 < db)
            def _():
                for cp in copies(b + 1, 0, 1 - slot):
                    cp.start()
        for cp in copies(b, c, slot):
            cp.wait()
        for j in range(chunk // PAGES_PER_BLOCK):
            tot = buf[slot, j * PAGES_PER_BLOCK]
            for g in range(1, PAGES_PER_BLOCK):
                tot = tot + buf[slot, j * PAGES_PER_BLOCK + g]
            blk = c * (chunk // PAGES_PER_BLOCK) + j
            kmt[:, blk:blk + 1] = jnp.sum(tot.reshape(D_ATTN, PAGE_SIZE), axis=-1, keepdims=True) * (1.0 / MOBA_BLOCK)

    n_blk = n_pages // PAGES_PER_BLOCK
    rows = N_HEADS * t_new
    q = q_ref[...]
    lane_head = lax.broadcasted_iota(jnp.int32, (t_new, D_ATTN), 1) // HEAD_DIM
    qbd = jnp.concatenate([jnp.where(lane_head == h, q, 0.0) for h in range(N_HEADS)], axis=0)
    gate = jnp.dot(qbd, kmt[...], precision=lax.Precision.HIGHEST, preferred_element_type=F32)
    col = lax.broadcasted_iota(jnp.int32, (rows, n_blk), 1)
    out_lane = lax.broadcasted_iota(jnp.int32, (rows, LANES), 1)
    out = jnp.zeros((rows, LANES), jnp.int32)
    for r in range(MOBA_TOPK):
        best = jnp.max(gate, axis=-1, keepdims=True)
        idx = jnp.min(jnp.where(gate == best, col, n_blk), axis=-1, keepdims=True)
        out = jnp.where(out_lane == r, idx, out)
        gate = jnp.where(col == idx, NEG_INF, gate)
    idx_ref[0] = out


def _select_blocks(page_table, q, cache_kt, *, layer, t_new):
    db, n_pages = page_table.shape
    n_blk = n_pages // PAGES_PER_BLOCK
    chunk = 16
    assert n_pages % (2 * chunk) == 0 and chunk % PAGES_PER_BLOCK == 0
    rows = N_HEADS * t_new
    grid_spec = pltpu.PrefetchScalarGridSpec(
        num_scalar_prefetch=1,
        grid=(db,),
        in_specs=[pl.BlockSpec((t_new, D_ATTN), lambda b, pt: (b, 0)),
                  pl.BlockSpec(memory_space=pl.ANY)],
        out_specs=pl.BlockSpec((1, rows, LANES), lambda b, pt: (b, 0, 0)),
        scratch_shapes=[pltpu.VMEM((2, chunk, N_HEADS, HEAD_DIM, PAGE_SIZE), F32),
                        pltpu.VMEM((D_ATTN, n_blk), F32),
                        pltpu.SemaphoreType.DMA((2,))],
    )
    return pl.pallas_call(
        functools.partial(_select_kernel, layer=layer, n_pages=n_pages, chunk=chunk, t_new=t_new, db=db),
        out_shape=jax.ShapeDtypeStruct((db, rows, LANES), jnp.int32),
        grid_spec=grid_spec,
        compiler_params=pltpu.CompilerParams(dimension_semantics=("arbitrary",), vmem_limit_bytes=VMEM_LIMIT),
        name="select_blocks",
    )(page_table.reshape(-1), q, cache_kt)


def _attn_sample_kernel(sel_ref, pt_ref, q_ref, kn_ref, vn_ref, ck_ref, cv_ref, o_ref, kbuf, vbuf, sems,
                        *, layer, n_pages, t_new, db):
    b = pl.program_id(0)
    n_slots = t_new * MOBA_TOPK
    scale = HEAD_DIM ** -0.5

    def copies(seq, h, par, slot):
        blk = sel_ref[(seq * N_HEADS + h) * n_slots + slot]
        out = []
        for g in range(PAGES_PER_BLOCK):
            page = pt_ref[seq * n_pages + blk * PAGES_PER_BLOCK + g]
            keys = pl.ds(g * PAGE_SIZE, PAGE_SIZE)
            out.append(pltpu.make_async_copy(ck_ref.at[layer, page, h], kbuf.at[par, slot, :, keys], sems.at[0, par]))
            out.append(pltpu.make_async_copy(cv_ref.at[layer, page, h], vbuf.at[par, slot, :, keys], sems.at[1, par]))
        return out

    def start_head(seq, h, par):
        def body(slot, carry):
            for cp in copies(seq, h, par, slot):
                cp.start()
            return carry
        lax.fori_loop(0, n_slots, body, 0)

    def wait_head(seq, h, par):
        def body(slot, carry):
            for cp in copies(seq, h, par, slot):
                cp.wait()
            return carry
        lax.fori_loop(0, n_slots, body, 0)

    @pl.when(b == 0)
    def _():
        start_head(0, 0, 0)

    row = lax.broadcasted_iota(jnp.int32, (t_new, MOBA_BLOCK), 0)
    tok = lax.broadcasted_iota(jnp.int32, (t_new, 1), 0)
    for h in range(N_HEADS):
        par = h % 2
        if h + 1 < N_HEADS:
            start_head(b, h + 1, 1 - par)
        else:
            @pl.when(b + 1 < db)
            def _():
                start_head(b + 1, 0, 1 - par)
        wait_head(b, h, par)

        head = slice(h * HEAD_DIM, (h + 1) * HEAD_DIM)
        q = q_ref[:, head] * scale
        qb16 = q.astype(BF16)

        s_all = [jnp.dot(qb16, kbuf[par, slot].astype(BF16), preferred_element_type=F32)
                 for slot in range(n_slots)]
        e = []
        for r in range(MOBA_TOPK):
            er = s_all[r]
            for t in range(1, t_new):
                er = jnp.where(row == t, s_all[t * MOBA_TOPK + r], er)
            e.append(er)
        kn, vn = kn_ref[:, head], vn_ref[:, head]
        s_new = [jnp.where(tok >= j, jnp.sum(q * kn[j:j + 1, :], axis=-1, keepdims=True), NEG_INF)
                 for j in range(t_new)]
        m = functools.reduce(jnp.maximum, s_new + [jnp.max(er, axis=-1, keepdims=True) for er in e])
        p = [jnp.exp(er - m) for er in e]
        p_new = [jnp.exp(s - m) for s in s_new]
        den = functools.reduce(jnp.add, p_new + [jnp.sum(pr, axis=-1, keepdims=True) for pr in p])
        num = functools.reduce(jnp.add, [p_new[j] * vn[j:j + 1, :] for j in range(t_new)])

        parts = [lax.dot_general(jnp.where(row == t, p[r], 0.0).astype(BF16),
                                 vbuf[par, t * MOBA_TOPK + r].astype(BF16), NT_DIMS, preferred_element_type=F32)
                 for t in range(t_new) for r in range(MOBA_TOPK)]
        o_ref[:, head] = (num + functools.reduce(jnp.add, parts)) / den


def _attn_sample(sel, page_table, q, k_new, v_new, cache_kt, cache_vt, *, layer, t_new):
    db, n_pages = page_table.shape
    n_slots = t_new * MOBA_TOPK
    tok = pl.BlockSpec((t_new, D_ATTN), lambda b, s, pt: (b, 0))
    grid_spec = pltpu.PrefetchScalarGridSpec(
        num_scalar_prefetch=2,
        grid=(db,),
        in_specs=[tok, tok, tok, pl.BlockSpec(memory_space=pl.ANY), pl.BlockSpec(memory_space=pl.ANY)],
        out_specs=tok,
        scratch_shapes=[pltpu.VMEM((2, n_slots, HEAD_DIM, MOBA_BLOCK), F32),
                        pltpu.VMEM((2, n_slots, HEAD_DIM, MOBA_BLOCK), F32),
                        pltpu.SemaphoreType.DMA((2, 2))],
    )
    return pl.pallas_call(
        functools.partial(_attn_sample_kernel, layer=layer, n_pages=n_pages, t_new=t_new, db=db),
        out_shape=jax.ShapeDtypeStruct((db * t_new, D_ATTN), F32),
        grid_spec=grid_spec,
        compiler_params=pltpu.CompilerParams(dimension_semantics=("arbitrary",), vmem_limit_bytes=VMEM_LIMIT),
        name="attn_sample",
    )(sel.reshape(-1), page_table.reshape(-1), q, k_new, v_new, cache_kt, cache_vt)


def _rope_tables(pos):
    half = HEAD_DIM // 2
    inv_freq = ROPE_THETA ** (-jnp.arange(half, dtype=F32) / half)
    ang = pos.astype(F32)[:, None] * inv_freq[None, :]
    cos, sin = jnp.cos(ang), jnp.sin(ang)
    cos = jnp.concatenate([cos, cos] * HEADS_PER_TILE, axis=-1)
    sin = jnp.concatenate([-sin, sin] * HEADS_PER_TILE, axis=-1)
    return cos, sin


def kernel(x_prompt, x_sample, cache_k, cache_v, state_conv, page_table, c_prompt, c_sample, w_ada, b_ada, g_mix, w_in, w_attn_o, conv_w, conv_b, conv_norm_g, conv_norm_b, w_conv_o, w_out, g_ffn, w_ffn_in, w_ffn_o, g_final):
    bsz, seq, _ = x_prompt.shape
    db, t_new, _ = x_sample.shape
    depth = w_in.shape[0]
    n_pages = page_table.shape[1]
    past = n_pages * PAGE_SIZE
    assert seq % MOBA_BLOCK == 0 and past % MOBA_BLOCK == 0 and t_new <= MOBA_BLOCK and t_new % SUBLANES == 0
    assert past // MOBA_BLOCK >= MOBA_TOPK
    assert cache_k.shape[2:] == (PAGE_SIZE, N_HEADS, HEAD_DIM)
    tm = 512 if seq % 512 == 0 else MOBA_BLOCK

    cos_p, sin_p = _rope_tables(jnp.arange(seq, dtype=jnp.int32))
    cos_s, sin_s = _rope_tables(past + jnp.arange(t_new, dtype=jnp.int32))
    cos_s, sin_s = jnp.tile(cos_s, (db, 1)), jnp.tile(sin_s, (db, 1))

    mod = _ada(jnp.concatenate([c_prompt, c_sample], axis=0), w_ada, b_ada)
    mod4 = mod.reshape(depth, bsz + db, 1, 6 * D_MODEL)
    w_in_b, w_attn_o_b, w_conv_o_b = w_in.astype(BF16), w_attn_o.astype(BF16), w_conv_o.astype(BF16)
    w_out_b, w_ffn_in_b, w_ffn_o_b = w_out.astype(BF16), w_ffn_in.astype(BF16), w_ffn_o.astype(BF16)
    row2 = lambda a, l: a[l][None, :]
    g_fin = g_final[None, :]
    by_token = lambda a: a.reshape(db, t_new, N_HEADS, HEAD_DIM)
    cache_kt = cache_k.transpose(0, 1, 3, 4, 2)
    cache_vt = cache_v.transpose(0, 1, 3, 4, 2)

    xp = x_prompt
    xs = x_sample.reshape(db * t_new, D_MODEL)
    kp, vp, cp, ksm, vsm, csm = [], [], [], [], [], []
    for l in range(depth):
        final = l == depth - 1
        lw = (row2(g_mix, l), w_in_b[l])
        cw = (conv_w[l], row2(conv_b, l), row2(conv_norm_g, l), row2(conv_norm_b, l))
        mw = (row2(g_ffn, l), g_fin, w_attn_o_b[l], w_conv_o_b[l], w_out_b[l], w_ffn_in_b[l], w_ffn_o_b[l])

        q, k, v, kb, vt, kmean, cbn, ga, gb, cs = _inproj_prompt(xp, mod4, l, *lw, cos_p, sin_p, *cw, tm=tm)
        attn = _attn_prompt(q, kb, vt, kmean.reshape(bsz, seq // MOBA_BLOCK, D_ATTN))
        flat = lambda a: a.reshape(bsz * seq, a.shape[-1])
        xp = _mlp(flat(xp), flat(attn), flat(cbn), flat(ga), flat(gb), mod[l, :bsz], *mw,
                  tm=tm, rows_per_mod=seq, final=final).reshape(bsz, seq, D_MODEL)
        kp.append(k.reshape(bsz, seq, N_HEADS, HEAD_DIM))
        vp.append(v.reshape(bsz, seq, N_HEADS, HEAD_DIM))
        cp.append(cs)

        mod_s = jnp.repeat(mod[l, bsz:], t_new, axis=0)
        qs, ks, vs, cbn_s, ga_s, gb_s, cs_s = _inproj_sample(xs, mod_s, mod_s, *lw, cos_s, sin_s, state_conv[l], *cw,
                                                             db=db, t_new=t_new)
        sel = _select_blocks(page_table, qs, cache_kt, layer=l, t_new=t_new)[:, :, :MOBA_TOPK]
        attn_s = _attn_sample(sel, page_table, qs, ks, vs, cache_kt, cache_vt, layer=l, t_new=t_new)
        xs = _mlp(xs, attn_s, cbn_s, ga_s, gb_s, mod_s, *mw, tm=db * t_new, rows_per_mod=1, final=final)
        ksm.append(by_token(ks))
        vsm.append(by_token(vs))
        csm.append(cs_s)

    return (xp, xs.reshape(db, t_new, D_MODEL), jnp.stack(kp), jnp.stack(vp), jnp.stack(cp),
            jnp.stack(ksm), jnp.stack(vsm), jnp.stack(csm))
```

```python
import functools

import jax
import jax.numpy as jnp
from jax import lax
from jax.experimental import pallas as pl
from jax.experimental.pallas import tpu as pltpu

D_MODEL = 1024
N_HEADS = 8
HEAD_DIM = 64
D_ATTN = N_HEADS * HEAD_DIM
D_CONV = D_MODEL // 2
CONV_K = 31
CONV_HALO = 32
MOBA_BLOCK = 256
MOBA_TOPK = 3
PAGE_SIZE = 128
PAGES_PER_BLOCK = MOBA_BLOCK // PAGE_SIZE
ROPE_THETA = 10000.0
D_FF = 2816
EPS = 1e-6
D_IN = 3 * D_ATTN + 2 * D_CONV + 2 * D_MODEL
LANES = 128
SUBLANES = 8
HEADS_PER_TILE = LANES // HEAD_DIM
N_HEAD_PAIRS = N_HEADS // HEADS_PER_TILE
FFN_CHUNK = 256
KEY_CHUNK_PAGES = 16
VMEM_LIMIT = 56 * 1024 * 1024

F32 = jnp.float32
BF16 = jnp.bfloat16
NEG_INF = float("-inf")
NT_DIMS = (((1,), (1,)), ((), ()))


def _resident(shape):
    return pl.BlockSpec(shape, lambda *_: (0,) * len(shape), pipeline_mode=pl.Buffered(1))


def _sigmoid(x):
    return 0.5 * jnp.tanh(0.5 * x) + 0.5


def _silu(x):
    return x * _sigmoid(x)


def _rms_mod(x, g, sc, sh):
    y = x * lax.rsqrt(jnp.mean(x * x, axis=-1, keepdims=True) + EPS) * g
    return y * (1.0 + sc) + sh


def _rope(y, cos, sin):
    reps = D_ATTN // LANES
    cos = jnp.concatenate([cos] * reps, axis=-1)
    sin = jnp.concatenate([sin] * reps, axis=-1)
    half = HEAD_DIM // 2
    lane = lax.broadcasted_iota(jnp.int32, y.shape, 1)
    first = (lane % HEAD_DIM) < half
    partner = jnp.where(first, pltpu.roll(y, D_ATTN - half, 1), pltpu.roll(y, half, 1))
    return y * cos + partner * sin


def _layer_norm_silu(c, g, b):
    mu = jnp.mean(c, axis=-1, keepdims=True)
    d = c - mu
    y = d * lax.rsqrt(jnp.mean(d * d, axis=-1, keepdims=True) + EPS) * g + b
    return _silu(y)


def _ada_kernel(c_ref, w_ref, b_ref, o_ref):
    c = c_ref[...]
    o_ref[0] = jnp.dot(_silu(c).astype(BF16), w_ref[0].astype(BF16), preferred_element_type=F32) + b_ref[0]


def _ada(c_all, w_ada, b_ada):
    depth, _, n_out = w_ada.shape
    rows = c_all.shape[0]
    tn = 1024
    return pl.pallas_call(
        _ada_kernel,
        out_shape=jax.ShapeDtypeStruct((depth, rows, n_out), F32),
        grid=(depth, n_out // tn),
        in_specs=[pl.BlockSpec((rows, D_MODEL), lambda l, j: (0, 0)),
                  pl.BlockSpec((1, D_MODEL, tn), lambda l, j: (l, 0, j)),
                  pl.BlockSpec((1, 1, tn), lambda l, j: (l, 0, j))],
        out_specs=pl.BlockSpec((1, rows, tn), lambda l, j: (l, 0, j)),
        compiler_params=pltpu.CompilerParams(dimension_semantics=("arbitrary", "arbitrary"),
                                             vmem_limit_bytes=VMEM_LIMIT),
        name="ada_mod",
    )(c_all, w_ada, b_ada.reshape(depth, 1, n_out))


def _inproj_prompt_kernel(x_ref, sh_ref, sc_ref, g_ref, w_ref, cos_ref, sin_ref, cw_ref, cb_ref, lng_ref, lnb_ref,
                          q_ref, k_ref, v_ref, kb_ref, vt_ref, km_ref, cbn_ref, ga_ref, gb_ref, cs_ref,
                          ubuf, zbuf, *, tm, nt):
    t = pl.program_id(1)
    hb = _rms_mod(x_ref[0], g_ref[...], sc_ref[0, 0], sh_ref[0, 0]).astype(BF16)

    def proj(lo, hi):
        return jnp.dot(hb, w_ref[:, lo:hi], preferred_element_type=F32)

    cos, sin = cos_ref[...], sin_ref[...]
    q_ref[0] = _rope(proj(0, D_ATTN), cos, sin)
    k = _rope(proj(D_ATTN, 2 * D_ATTN), cos, sin)
    k_ref[0] = k
    kb_ref[0] = k.astype(BF16)
    for j in range(tm // MOBA_BLOCK):
        km_ref[0, j] = jnp.sum(k[j * MOBA_BLOCK:(j + 1) * MOBA_BLOCK], axis=0, keepdims=True) * (1.0 / MOBA_BLOCK)
    v = proj(2 * D_ATTN, 3 * D_ATTN)
    v_ref[0] = v
    for j in range(tm // MOBA_BLOCK):
        vt_ref[0, j] = v[j * MOBA_BLOCK:(j + 1) * MOBA_BLOCK].T.astype(BF16)
    u0 = 3 * D_ATTN
    glu = proj(u0, u0 + D_CONV) * _sigmoid(proj(u0 + D_CONV, u0 + 2 * D_CONV))
    g0 = u0 + 2 * D_CONV
    ga_ref[0] = _sigmoid(proj(g0, g0 + D_MODEL)).astype(BF16)
    gb_ref[0] = _sigmoid(proj(g0 + D_MODEL, g0 + 2 * D_MODEL)).astype(BF16)

    @pl.when(t == 0)
    def _():
        ubuf[0:CONV_HALO, :] = jnp.zeros((CONV_HALO, D_CONV), F32)

    ubuf[CONV_HALO:CONV_HALO + tm, :] = glu
    acc = jnp.zeros((tm, D_CONV), F32) + cb_ref[...]
    for r in range(SUBLANES):
        z = None
        for a in range(-(-CONV_K // SUBLANES)):
            d = SUBLANES * a + r
            if d < CONV_K:
                term = (cw_ref[pl.ds(CONV_K - 1 - d, 1), :]
                        * ubuf[pl.ds(CONV_HALO - SUBLANES * (a + 1), tm + SUBLANES), :])
                z = term if z is None else z + term
        zbuf[...] = z
        acc = acc + zbuf[pl.ds(SUBLANES - r, tm), :]
    cbn_ref[0] = _layer_norm_silu(acc, lng_ref[...], lnb_ref[...]).astype(BF16)

    @pl.when(t == nt - 1)
    def _():
        cs_ref[0] = ubuf[pl.ds(CONV_HALO + tm - (CONV_K - 1), CONV_K - 1), :]

    ubuf[0:CONV_HALO, :] = ubuf[tm:tm + CONV_HALO, :]


def _inproj_prompt(x, mod4, l, g_mix, w_in, cos, sin, conv_w, conv_b, ln_g, ln_b, *, tm):
    bsz, s, _ = x.shape
    nt = s // tm
    nb = s // MOBA_BLOCK
    row = lambda b, t: (b, t, 0)
    vec = lambda n: pl.BlockSpec((1, n), lambda b, t: (0, 0))
    mod = lambda j: pl.BlockSpec((1, 1, 1, D_MODEL), lambda b, t: (l, b, 0, j))
    tok = lambda n: pl.BlockSpec((1, tm, n), row)
    out_shape = (
        jax.ShapeDtypeStruct((bsz, s, D_ATTN), F32),
        jax.ShapeDtypeStruct((bsz, s, D_ATTN), F32),
        jax.ShapeDtypeStruct((bsz, s, D_ATTN), F32),
        jax.ShapeDtypeStruct((bsz, s, D_ATTN), BF16),
        jax.ShapeDtypeStruct((bsz, nb, D_ATTN, MOBA_BLOCK), BF16),
        jax.ShapeDtypeStruct((bsz, nb, 1, D_ATTN), F32),
        jax.ShapeDtypeStruct((bsz, s, D_CONV), BF16),
        jax.ShapeDtypeStruct((bsz, s, D_MODEL), BF16),
        jax.ShapeDtypeStruct((bsz, s, D_MODEL), BF16),
        jax.ShapeDtypeStruct((bsz, CONV_K - 1, D_CONV), F32),
    )
    out_specs = (tok(D_ATTN), tok(D_ATTN), tok(D_ATTN), tok(D_ATTN),
                 pl.BlockSpec((1, tm // MOBA_BLOCK, D_ATTN, MOBA_BLOCK), lambda b, t: (b, t, 0, 0)),
                 pl.BlockSpec((1, tm // MOBA_BLOCK, 1, D_ATTN), lambda b, t: (b, t, 0, 0)),
                 tok(D_CONV), tok(D_MODEL), tok(D_MODEL),
                 pl.BlockSpec((1, CONV_K - 1, D_CONV), lambda b, t: (b, 0, 0)))
    return pl.pallas_call(
        functools.partial(_inproj_prompt_kernel, tm=tm, nt=nt),
        out_shape=out_shape,
        grid=(bsz, nt),
        in_specs=[tok(D_MODEL), mod(0), mod(1), vec(D_MODEL), _resident((D_MODEL, D_IN)),
                  pl.BlockSpec((tm, LANES), lambda b, t: (t, 0)), pl.BlockSpec((tm, LANES), lambda b, t: (t, 0)),
                  pl.BlockSpec((CONV_K, D_CONV), lambda b, t: (0, 0)), vec(D_CONV), vec(D_CONV), vec(D_CONV)],
        out_specs=out_specs,
        scratch_shapes=[pltpu.VMEM((CONV_HALO + tm, D_CONV), F32), pltpu.VMEM((tm + SUBLANES, D_CONV), F32)],
        compiler_params=pltpu.CompilerParams(dimension_semantics=("arbitrary", "arbitrary"),
                                             vmem_limit_bytes=VMEM_LIMIT),
        name="inproj_prompt",
    )(x, mod4, mod4, g_mix, w_in, cos, sin, conv_w, conv_b, ln_g, ln_b)


def _head_rows_mask(rows):
    lane_head = lax.broadcasted_iota(jnp.int32, (rows, D_ATTN), 1) // HEAD_DIM
    return lane_head == lax.broadcasted_iota(jnp.int32, (rows, D_ATTN), 0) % N_HEADS


def _attn_prompt_kernel(q_ref, k_ref, vt_ref, km_ref, o_ref, qt_ref, sel_ref, *, nb):
    qb = pl.program_id(1)
    tq = MOBA_BLOCK
    q = q_ref[0]

    hm = _head_rows_mask(N_HEADS)
    km = km_ref[0]
    kmt = jnp.concatenate([jnp.where(hm, km[n:n + 1, :], 0.0) for n in range(nb)], axis=0)
    gate = lax.dot_general(kmt, q, NT_DIMS, precision=lax.Precision.HIGHEST, preferred_element_type=F32)
    g = [gate[n * N_HEADS:(n + 1) * N_HEADS, :] for n in range(nb)]
    for n in range(nb - 1):
        rank = jnp.zeros((N_HEADS, tq), F32)
        for m in range(nb - 1):
            if m != n:
                beats = (g[m] >= g[n]) if m < n else (g[m] > g[n])
                rank = rank + jnp.where(beats, (m < qb).astype(F32), 0.0)
        sel_ref[n] = jnp.where(rank < MOBA_TOPK, 1.0, 0.0)

    qt = (q * HEAD_DIM ** -0.5).T
    sub = lax.broadcasted_iota(jnp.int32, (LANES, tq), 0)
    for p in range(N_HEAD_PAIRS):
        pair = qt[p * LANES:(p + 1) * LANES, :]
        qt_ref[2 * p] = jnp.where(sub < HEAD_DIM, pair, 0.0).astype(BF16)
        qt_ref[2 * p + 1] = jnp.where(sub < HEAD_DIM, 0.0, pair).astype(BF16)

    def block(n, start, mask_of, state):
        ms, ls, accs = (list(s) for s in state) if state is not None else (None, None, None)
        scores = []
        for p in range(N_HEAD_PAIRS):
            kk = k_ref[0, pl.ds(start, MOBA_BLOCK), p * LANES:(p + 1) * LANES]
            for hh in range(HEADS_PER_TILE):
                scores.append(jnp.dot(kk, qt_ref[HEADS_PER_TILE * p + hh], preferred_element_type=F32))
        out_m, out_l, out_acc = [], [], []
        for h in range(N_HEADS):
            s = jnp.where(mask_of(h), scores[h], NEG_INF)
            m_blk = jnp.max(s, axis=0, keepdims=True)
            m_new = m_blk if state is None else jnp.maximum(ms[h], m_blk)
            e = jnp.exp(s - m_new)
            l_new = jnp.sum(e, axis=0, keepdims=True)
            pv = jnp.dot(vt_ref[0, n, h * HEAD_DIM:(h + 1) * HEAD_DIM, :], e.astype(BF16),
                         preferred_element_type=F32)
            if state is not None:
                alpha = jnp.exp(ms[h] - m_new)
                l_new = alpha * ls[h] + l_new
                pv = alpha * accs[h] + pv
            out_m.append(m_new)
            out_l.append(l_new)
            out_acc.append(pv)
        return tuple(out_m), tuple(out_l), tuple(out_acc)

    causal = (lax.broadcasted_iota(jnp.int32, (tq, tq), 0) <= lax.broadcasted_iota(jnp.int32, (tq, tq), 1))
    state = block(qb, pl.multiple_of(qb * MOBA_BLOCK, MOBA_BLOCK), lambda h: causal, None)

    def past_block(n, state):
        return block(n, pl.multiple_of(n * MOBA_BLOCK, MOBA_BLOCK), lambda h: sel_ref[n, h:h + 1, :] > 0.5, state)

    _, ls, accs = lax.fori_loop(0, qb, past_block, state)
    out = jnp.concatenate([accs[h] / ls[h] for h in range(N_HEADS)], axis=0)
    o_ref[0] = out.T.astype(BF16)


def _attn_prompt(q, kb, vt, kmean):
    bsz, s, _ = q.shape
    nb = s // MOBA_BLOCK
    return pl.pallas_call(
        functools.partial(_attn_prompt_kernel, nb=nb),
        out_shape=jax.ShapeDtypeStruct((bsz, s, D_ATTN), BF16),
        grid=(bsz, nb),
        in_specs=[pl.BlockSpec((1, MOBA_BLOCK, D_ATTN), lambda b, i: (b, i, 0)),
                  pl.BlockSpec((1, s, D_ATTN), lambda b, i: (b, 0, 0)),
                  pl.BlockSpec((1, nb, D_ATTN, MOBA_BLOCK), lambda b, i: (b, 0, 0, 0)),
                  pl.BlockSpec((1, nb, D_ATTN), lambda b, i: (b, 0, 0))],
        out_specs=pl.BlockSpec((1, MOBA_BLOCK, D_ATTN), lambda b, i: (b, i, 0)),
        scratch_shapes=[pltpu.VMEM((N_HEADS, LANES, MOBA_BLOCK), BF16),
                        pltpu.VMEM((max(nb - 1, 1), N_HEADS, MOBA_BLOCK), F32)],
        compiler_params=pltpu.CompilerParams(dimension_semantics=("arbitrary", "arbitrary"),
                                             vmem_limit_bytes=VMEM_LIMIT),
        name="attn_prompt",
    )(q, kb, vt, kmean)


N_MLP_INPUTS = 16


def _key_mean_streamer(pt_ref, ck_ref, kmt_ref, buf, sems, *, layer, n_pages, n_seq):
    seq = pl.program_id(0)
    n_chunks = n_pages // KEY_CHUNK_PAGES
    blocks_per_chunk = KEY_CHUNK_PAGES // PAGES_PER_BLOCK

    def copies(s, c, slot):
        return [pltpu.make_async_copy(ck_ref.at[layer, pt_ref[s * n_pages + c * KEY_CHUNK_PAGES + g]],
                                      buf.at[slot, g], sems.at[slot]) for g in range(KEY_CHUNK_PAGES)]

    @pl.when(seq == 0)
    def _():
        for cp in copies(0, 0, 0):
            cp.start()

    def do_chunk(c):
        slot = c % 2
        if c + 1 < n_chunks:
            for cp in copies(seq, c + 1, 1 - slot):
                cp.start()
        else:
            @pl.when(seq + 1 < n_seq)
            def _():
                for cp in copies(seq + 1, 0, 1 - slot):
                    cp.start()
        for cp in copies(seq, c, slot):
            cp.wait()
        for j in range(blocks_per_chunk):
            tot = buf[slot, j * PAGES_PER_BLOCK]
            for g in range(1, PAGES_PER_BLOCK):
                tot = tot + buf[slot, j * PAGES_PER_BLOCK + g]
            blk = c * blocks_per_chunk + j
            kmt_ref[0, :, blk:blk + 1] = (jnp.sum(tot.reshape(D_ATTN, PAGE_SIZE), axis=-1, keepdims=True)
                                          * (1.0 / MOBA_BLOCK))

    return n_chunks, do_chunk


def _mlp_kernel(*refs, final, stream):
    if stream is None:
        ins, (o_ref,) = refs[:N_MLP_INPUTS], refs[N_MLP_INPUTS:]
        n_chunks, do_chunk = 0, None
    else:
        pt_ref, ins, ck_ref = refs[0], refs[1:1 + N_MLP_INPUTS], refs[1 + N_MLP_INPUTS]
        o_ref, kmt_ref, buf, sems = refs[2 + N_MLP_INPUTS:]
        layer, n_pages, n_seq = stream
        n_chunks, do_chunk = _key_mean_streamer(pt_ref, ck_ref, kmt_ref, buf, sems, layer=layer, n_pages=n_pages,
                                                n_seq=n_seq)
    (x_ref, a_ref, cbn_ref, ga_ref, gb_ref, gtm_ref, shf_ref, scf_ref, gtf_ref, gffn_ref, gfin_ref,
     wao_ref, wco_ref, wout_ref, wfi_ref, wfo_ref) = ins
    ffn_steps = list(range(0, D_FF, FFN_CHUNK))
    spread = max(n_chunks - 2, 0)
    at_step = {((i + 1) * len(ffn_steps)) // (spread + 1): 2 + i for i in range(spread)}
    assert len(at_step) == spread

    if n_chunks > 0:
        do_chunk(0)
    a = jnp.dot(a_ref[...].astype(BF16), wao_ref[...], preferred_element_type=F32)
    cb = jnp.dot(cbn_ref[...], wco_ref[...], preferred_element_type=F32)
    merged = ga_ref[...].astype(F32) * a + gb_ref[...].astype(F32) * cb
    x = x_ref[...] + gtm_ref[...] * jnp.dot(merged.astype(BF16), wout_ref[...], preferred_element_type=F32)
    hb = _rms_mod(x, gffn_ref[...], scf_ref[...], shf_ref[...]).astype(BF16)
    if n_chunks > 1:
        do_chunk(1)

    def gate_up(c):
        return (jnp.dot(hb, wfi_ref[:, c:c + FFN_CHUNK], preferred_element_type=F32),
                jnp.dot(hb, wfi_ref[:, D_FF + c:D_FF + c + FFN_CHUNK], preferred_element_type=F32))

    acc = jnp.zeros(x.shape, F32)
    nxt = gate_up(0)
    for i, c in enumerate(ffn_steps):
        gate, up = nxt
        if c + FFN_CHUNK < D_FF:
            nxt = gate_up(c + FFN_CHUNK)
        acc = acc + jnp.dot((_silu(gate) * up).astype(BF16), wfo_ref[c:c + FFN_CHUNK, :], preferred_element_type=F32)
        if i in at_step:
            do_chunk(at_step[i])
    x = x + gtf_ref[...] * acc
    if final:
        x = x * lax.rsqrt(jnp.mean(x * x, axis=-1, keepdims=True) + EPS) * gfin_ref[...]
    o_ref[...] = x


def _mlp(x, attn, cbn, ga, gb, mod_rows, g_ffn, g_final, w_attn_o, w_conv_o, w_out, w_ffn_in, w_ffn_o,
         *, tm, rows_per_mod, final, key_stream=None):
    m = x.shape[0]
    tok = lambda n: pl.BlockSpec((tm, n), lambda i, *_: (i, 0))
    vec = lambda n: pl.BlockSpec((1, n), lambda i, *_: (0, 0))
    if rows_per_mod == 1:
        mod = lambda j: pl.BlockSpec((tm, D_MODEL), lambda i, *_: (i, j))
    else:
        assert rows_per_mod % tm == 0
        mod_rows = mod_rows.reshape(mod_rows.shape[0], 1, 6 * D_MODEL)
        mod = lambda j: pl.BlockSpec((None, 1, D_MODEL), lambda i, *_: (i // (rows_per_mod // tm), 0, j))
    in_specs = [tok(D_MODEL), tok(D_ATTN), tok(D_CONV), tok(D_MODEL), tok(D_MODEL),
                mod(2), mod(3), mod(4), mod(5), vec(D_MODEL), vec(D_MODEL),
                _resident((D_ATTN, D_MODEL)), _resident((D_CONV, D_MODEL)), _resident((D_MODEL, D_MODEL)),
                _resident((D_MODEL, 2 * D_FF)), _resident((D_FF, D_MODEL))]
    assert len(in_specs) == N_MLP_INPUTS
    args = (x, attn, cbn, ga, gb, mod_rows, mod_rows, mod_rows, mod_rows, g_ffn, g_final,
            w_attn_o, w_conv_o, w_out, w_ffn_in, w_ffn_o)
    params = pltpu.CompilerParams(dimension_semantics=("arbitrary",), vmem_limit_bytes=VMEM_LIMIT)
    x_shape = jax.ShapeDtypeStruct((m, D_MODEL), F32)
    if key_stream is None:
        return pl.pallas_call(
            functools.partial(_mlp_kernel, final=final, stream=None),
            out_shape=x_shape, grid=(m // tm,), in_specs=in_specs, out_specs=tok(D_MODEL),
            compiler_params=params, name="mixer_out_ffn",
        )(*args)
    page_table, cache_kt, layer = key_stream
    db, n_pages = page_table.shape
    n_blk = n_pages // PAGES_PER_BLOCK
    assert m // tm == db, "one sample sequence's keys are streamed per grid step"
    assert n_pages % (2 * KEY_CHUNK_PAGES) == 0 and KEY_CHUNK_PAGES % PAGES_PER_BLOCK == 0
    grid_spec = pltpu.PrefetchScalarGridSpec(
        num_scalar_prefetch=1,
        grid=(m // tm,),
        in_specs=in_specs + [pl.BlockSpec(memory_space=pl.ANY)],
        out_specs=(tok(D_MODEL), pl.BlockSpec((1, D_ATTN, n_blk), lambda i, *_: (i, 0, 0))),
        scratch_shapes=[pltpu.VMEM((2, KEY_CHUNK_PAGES, N_HEADS, HEAD_DIM, PAGE_SIZE), F32),
                        pltpu.SemaphoreType.DMA((2,))],
    )
    return pl.pallas_call(
        functools.partial(_mlp_kernel, final=final, stream=(layer, n_pages, db)),
        out_shape=(x_shape, jax.ShapeDtypeStruct((db, D_ATTN, n_blk), F32)),
        grid_spec=grid_spec, compiler_params=params, name="mixer_out_ffn_keys",
    )(page_table.reshape(-1), *args, cache_kt)


def _inproj_sample_kernel(x_ref, sh_ref, sc_ref, g_ref, w_ref, cos_ref, sin_ref, st_ref, cw_ref, cb_ref, lng_ref,
                          lnb_ref, q_ref, k_ref, v_ref, cbn_ref, ga_ref, gb_ref, cs_ref, uc, *, db, t_new):
    hb = _rms_mod(x_ref[...], g_ref[...], sc_ref[...], sh_ref[...]).astype(BF16)

    def proj(lo, hi):
        return jnp.dot(hb, w_ref[:, lo:hi], preferred_element_type=F32)

    cos, sin = cos_ref[...], sin_ref[...]
    q_ref[...] = _rope(proj(0, D_ATTN), cos, sin)
    k_ref[...] = _rope(proj(D_ATTN, 2 * D_ATTN), cos, sin)
    v_ref[...] = proj(2 * D_ATTN, 3 * D_ATTN)
    u0 = 3 * D_ATTN
    glu = proj(u0, u0 + D_CONV) * _sigmoid(proj(u0 + D_CONV, u0 + 2 * D_CONV))
    g0 = u0 + 2 * D_CONV
    ga_ref[...] = _sigmoid(proj(g0, g0 + D_MODEL)).astype(BF16)
    gb_ref[...] = _sigmoid(proj(g0 + D_MODEL, g0 + 2 * D_MODEL)).astype(BF16)

    ctx = CONV_K - 1
    uc[:, 0:ctx, :] = st_ref[...]
    uc[:, ctx:ctx + t_new, :] = glu.reshape(db, t_new, D_CONV)
    acc = jnp.zeros((db, t_new, D_CONV), F32) + cb_ref[...]
    for j in range(CONV_K):
        acc = acc + cw_ref[pl.ds(j, 1), :] * uc[:, pl.ds(j, t_new), :]
    cbn_ref[...] = _layer_norm_silu(acc.reshape(db * t_new, D_CONV), lng_ref[...], lnb_ref[...]).astype(BF16)
    cs_ref[...] = uc[:, pl.ds(t_new, ctx), :]


def _inproj_sample(x, sh, sc, g_mix, w_in, cos, sin, state, conv_w, conv_b, ln_g, ln_b, *, db, t_new):
    m = db * t_new
    full = lambda *shape: pl.BlockSpec(shape, lambda i: (0,) * len(shape))
    out_shape = (
        jax.ShapeDtypeStruct((m, D_ATTN), F32), jax.ShapeDtypeStruct((m, D_ATTN), F32),
        jax.ShapeDtypeStruct((m, D_ATTN), F32), jax.ShapeDtypeStruct((m, D_CONV), BF16),
        jax.ShapeDtypeStruct((m, D_MODEL), BF16), jax.ShapeDtypeStruct((m, D_MODEL), BF16),
        jax.ShapeDtypeStruct((db, CONV_K - 1, D_CONV), F32),
    )
    ctx_rows = -(-(CONV_K - 1 + t_new) // SUBLANES) * SUBLANES
    return pl.pallas_call(
        functools.partial(_inproj_sample_kernel, db=db, t_new=t_new),
        out_shape=out_shape,
        grid=(1,),
        in_specs=[full(m, D_MODEL), pl.BlockSpec((m, D_MODEL), lambda i: (0, 0)),
                  pl.BlockSpec((m, D_MODEL), lambda i: (0, 1)), full(1, D_MODEL), full(D_MODEL, D_IN),
                  full(m, LANES), full(m, LANES), full(db, CONV_K - 1, D_CONV), full(CONV_K, D_CONV),
                  full(1, D_CONV), full(1, D_CONV), full(1, D_CONV)],
        out_specs=(full(m, D_ATTN), full(m, D_ATTN), full(m, D_ATTN), full(m, D_CONV), full(m, D_MODEL),
                   full(m, D_MODEL), full(db, CONV_K - 1, D_CONV)),
        scratch_shapes=[pltpu.VMEM((db, ctx_rows, D_CONV), F32)],
        compiler_params=pltpu.CompilerParams(dimension_semantics=("arbitrary",), vmem_limit_bytes=VMEM_LIMIT),
        name="inproj_sample",
    )(x, sh, sc, g_mix, w_in, cos, sin, state, conv_w, conv_b, ln_g, ln_b)


def _select_kernel(q_ref, kmt_ref, idx_ref, *, t_new):
    n_blk = kmt_ref.shape[-1]
    rows = N_HEADS * t_new
    q = q_ref[...]
    lane_head = lax.broadcasted_iota(jnp.int32, (t_new, D_ATTN), 1) // HEAD_DIM
    qbd = jnp.concatenate([jnp.where(lane_head == h, q, 0.0) for h in range(N_HEADS)], axis=0)
    gate = jnp.dot(qbd, kmt_ref[0], precision=lax.Precision.HIGHEST, preferred_element_type=F32)
    col = lax.broadcasted_iota(jnp.int32, (rows, n_blk), 1)
    out_lane = lax.broadcasted_iota(jnp.int32, (rows, LANES), 1)
    out = jnp.zeros((rows, LANES), jnp.int32)
    for r in range(MOBA_TOPK):
        best = jnp.max(gate, axis=-1, keepdims=True)
        idx = jnp.min(jnp.where(gate == best, col, n_blk), axis=-1, keepdims=True)
        out = jnp.where(out_lane == r, idx, out)
        gate = jnp.where(col == idx, NEG_INF, gate)
    idx_ref[0] = out


def _select_blocks(q, key_means, *, t_new):
    db, _, n_blk = key_means.shape
    rows = N_HEADS * t_new
    return pl.pallas_call(
        functools.partial(_select_kernel, t_new=t_new),
        out_shape=jax.ShapeDtypeStruct((db, rows, LANES), jnp.int32),
        grid=(db,),
        in_specs=[pl.BlockSpec((t_new, D_ATTN), lambda b: (b, 0)),
                  pl.BlockSpec((1, D_ATTN, n_blk), lambda b: (b, 0, 0))],
        out_specs=pl.BlockSpec((1, rows, LANES), lambda b: (b, 0, 0)),
        compiler_params=pltpu.CompilerParams(dimension_semantics=("arbitrary",), vmem_limit_bytes=VMEM_LIMIT),
        name="select_blocks",
    )(q, key_means)


def _attn_sample_kernel(sel_ref, pt_ref, q_ref, kn_ref, vn_ref, ck_ref, cv_ref, o_ref, kbuf, vbuf, sems,
                        *, layer, n_pages, t_new, db):
    b = pl.program_id(0)
    n_slots = t_new * MOBA_TOPK
    scale = HEAD_DIM ** -0.5

    def copies(seq, h, par, slot):
        blk = sel_ref[(seq * N_HEADS + h) * n_slots + slot]
        out = []
        for g in range(PAGES_PER_BLOCK):
            page = pt_ref[seq * n_pages + blk * PAGES_PER_BLOCK + g]
            keys = pl.ds(g * PAGE_SIZE, PAGE_SIZE)
            out.append(pltpu.make_async_copy(ck_ref.at[layer, page, h], kbuf.at[par, slot, :, keys], sems.at[0, par]))
            out.append(pltpu.make_async_copy(cv_ref.at[layer, page, h], vbuf.at[par, slot, :, keys], sems.at[1, par]))
        return out

    def start_head(seq, h, par):
        for slot in range(n_slots):
            for cp in copies(seq, h, par, slot):
                cp.start()

    def wait_head(par):
        pltpu.make_async_copy(kbuf.at[par], kbuf.at[par], sems.at[0, par]).wait()
        pltpu.make_async_copy(vbuf.at[par], vbuf.at[par], sems.at[1, par]).wait()

    @pl.when(b == 0)
    def _():
        start_head(0, 0, 0)

    row = lax.broadcasted_iota(jnp.int32, (t_new, MOBA_BLOCK), 0)
    tok = lax.broadcasted_iota(jnp.int32, (t_new, 1), 0)
    for h in range(N_HEADS):
        par = h % 2
        if h + 1 < N_HEADS:
            start_head(b, h + 1, 1 - par)
        else:
            @pl.when(b + 1 < db)
            def _():
                start_head(b + 1, 0, 1 - par)
        wait_head(par)

        head = slice(h * HEAD_DIM, (h + 1) * HEAD_DIM)
        q = q_ref[:, head] * scale
        qb16 = q.astype(BF16)

        s_all = [jnp.dot(qb16, kbuf[par, slot].astype(BF16), preferred_element_type=F32)
                 for slot in range(n_slots)]
        e = []
        for r in range(MOBA_TOPK):
            er = s_all[r]
            for t in range(1, t_new):
                er = jnp.where(row == t, s_all[t * MOBA_TOPK + r], er)
            e.append(er)
        kn, vn = kn_ref[:, head], vn_ref[:, head]
        s_new = [jnp.where(tok >= j, jnp.sum(q * kn[j:j + 1, :], axis=-1, keepdims=True), NEG_INF)
                 for j in range(t_new)]
        m = functools.reduce(jnp.maximum, s_new + [jnp.max(er, axis=-1, keepdims=True) for er in e])
        p = [jnp.exp(er - m) for er in e]
        p_new = [jnp.exp(s - m) for s in s_new]
        den = functools.reduce(jnp.add, p_new + [jnp.sum(pr, axis=-1, keepdims=True) for pr in p])
        num = functools.reduce(jnp.add, [p_new[j] * vn[j:j + 1, :] for j in range(t_new)])

        parts = [lax.dot_general(jnp.where(row == t, p[r], 0.0).astype(BF16),
                                 vbuf[par, t * MOBA_TOPK + r].astype(BF16), NT_DIMS, preferred_element_type=F32)
                 for t in range(t_new) for r in range(MOBA_TOPK)]
        o_ref[:, head] = (num + functools.reduce(jnp.add, parts)) / den


def _attn_sample(sel, page_table, q, k_new, v_new, cache_kt, cache_vt, *, layer, t_new):
    db, n_pages = page_table.shape
    n_slots = t_new * MOBA_TOPK
    tok = pl.BlockSpec((t_new, D_ATTN), lambda b, s, pt: (b, 0))
    grid_spec = pltpu.PrefetchScalarGridSpec(
        num_scalar_prefetch=2,
        grid=(db,),
        in_specs=[tok, tok, tok, pl.BlockSpec(memory_space=pl.ANY), pl.BlockSpec(memory_space=pl.ANY)],
        out_specs=tok,
        scratch_shapes=[pltpu.VMEM((2, n_slots, HEAD_DIM, MOBA_BLOCK), F32),
                        pltpu.VMEM((2, n_slots, HEAD_DIM, MOBA_BLOCK), F32),
                        pltpu.SemaphoreType.DMA((2, 2))],
    )
    return pl.pallas_call(
        functools.partial(_attn_sample_kernel, layer=layer, n_pages=n_pages, t_new=t_new, db=db),
        out_shape=jax.ShapeDtypeStruct((db * t_new, D_ATTN), F32),
        grid_spec=grid_spec,
        compiler_params=pltpu.CompilerParams(dimension_semantics=("arbitrary",), vmem_limit_bytes=VMEM_LIMIT),
        name="attn_sample",
    )(sel.reshape(-1), page_table.reshape(-1), q, k_new, v_new, cache_kt, cache_vt)


def _rope_tables(pos):
    half = HEAD_DIM // 2
    inv_freq = ROPE_THETA ** (-jnp.arange(half, dtype=F32) / half)
    ang = pos.astype(F32)[:, None] * inv_freq[None, :]
    cos, sin = jnp.cos(ang), jnp.sin(ang)
    cos = jnp.concatenate([cos, cos] * HEADS_PER_TILE, axis=-1)
    sin = jnp.concatenate([-sin, sin] * HEADS_PER_TILE, axis=-1)
    return cos, sin


def kernel(x_prompt, x_sample, cache_k, cache_v, state_conv, page_table, c_prompt, c_sample, w_ada, b_ada, g_mix, w_in, w_attn_o, conv_w, conv_b, conv_norm_g, conv_norm_b, w_conv_o, w_out, g_ffn, w_ffn_in, w_ffn_o, g_final):
    bsz, seq, _ = x_prompt.shape
    db, t_new, _ = x_sample.shape
    depth = w_in.shape[0]
    n_pages = page_table.shape[1]
    past = n_pages * PAGE_SIZE
    assert seq % MOBA_BLOCK == 0 and past % MOBA_BLOCK == 0 and t_new <= MOBA_BLOCK and t_new % SUBLANES == 0
    assert past // MOBA_BLOCK >= MOBA_TOPK
    assert cache_k.shape[2:] == (PAGE_SIZE, N_HEADS, HEAD_DIM)
    tm = 512 if seq % 512 == 0 else MOBA_BLOCK

    cos_p, sin_p = _rope_tables(jnp.arange(seq, dtype=jnp.int32))
    cos_s, sin_s = _rope_tables(past + jnp.arange(t_new, dtype=jnp.int32))
    cos_s, sin_s = jnp.tile(cos_s, (db, 1)), jnp.tile(sin_s, (db, 1))

    mod = _ada(jnp.concatenate([c_prompt, c_sample], axis=0), w_ada, b_ada)
    mod4 = mod.reshape(depth, bsz + db, 1, 6 * D_MODEL)
    w_in_b, w_attn_o_b, w_conv_o_b = w_in.astype(BF16), w_attn_o.astype(BF16), w_conv_o.astype(BF16)
    w_out_b, w_ffn_in_b, w_ffn_o_b = w_out.astype(BF16), w_ffn_in.astype(BF16), w_ffn_o.astype(BF16)
    row2 = lambda a, l: a[l][None, :]
    g_fin = g_final[None, :]
    by_token = lambda a: a.reshape(db, t_new, N_HEADS, HEAD_DIM)
    cache_kt = cache_k.transpose(0, 1, 3, 4, 2)
    cache_vt = cache_v.transpose(0, 1, 3, 4, 2)

    xp = x_prompt
    xs = x_sample.reshape(db * t_new, D_MODEL)
    kp, vp, cp, ksm, vsm, csm = [], [], [], [], [], []
    for l in range(depth):
        final = l == depth - 1
        lw = (row2(g_mix, l), w_in_b[l])
        cw = (conv_w[l], row2(conv_b, l), row2(conv_norm_g, l), row2(conv_norm_b, l))
        mw = (row2(g_ffn, l), g_fin, w_attn_o_b[l], w_conv_o_b[l], w_out_b[l], w_ffn_in_b[l], w_ffn_o_b[l])

        q, k, v, kb, vt, kmean, cbn, ga, gb, cs = _inproj_prompt(xp, mod4, l, *lw, cos_p, sin_p, *cw, tm=tm)
        attn = _attn_prompt(q, kb, vt, kmean.reshape(bsz, seq // MOBA_BLOCK, D_ATTN))
        flat = lambda a: a.reshape(bsz * seq, a.shape[-1])
        xp, key_means = _mlp(flat(xp), flat(attn), flat(cbn), flat(ga), flat(gb), mod[l, :bsz], *mw,
                             tm=tm, rows_per_mod=seq, final=final, key_stream=(page_table, cache_kt, l))
        xp = xp.reshape(bsz, seq, D_MODEL)
        kp.append(k.reshape(bsz, seq, N_HEADS, HEAD_DIM))
        vp.append(v.reshape(bsz, seq, N_HEADS, HEAD_DIM))
        cp.append(cs)

        mod_s = jnp.repeat(mod[l, bsz:], t_new, axis=0)
        qs, ks, vs, cbn_s, ga_s, gb_s, cs_s = _inproj_sample(xs, mod_s, mod_s, *lw, cos_s, sin_s, state_conv[l], *cw,
                                                             db=db, t_new=t_new)
        sel = _select_blocks(qs, key_means, t_new=t_new)[:, :, :MOBA_TOPK]
        attn_s = _attn_sample(sel, page_table, qs, ks, vs, cache_kt, cache_vt, layer=l, t_new=t_new)
        xs = _mlp(xs, attn_s, cbn_s, ga_s, gb_s, mod_s, *mw, tm=db * t_new, rows_per_mod=1, final=final)
        ksm.append(by_token(ks))
        vsm.append(by_token(vs))
        csm.append(cs_s)

    return (xp, xs.reshape(db, t_new, D_MODEL), jnp.stack(kp), jnp.stack(vp), jnp.stack(cp),
            jnp.stack(ksm), jnp.stack(vsm), jnp.stack(csm))
```

```python
import functools

import jax
import jax.numpy as jnp
from jax import lax
from jax.experimental import pallas as pl
from jax.experimental.pallas import tpu as pltpu

D_MODEL = 1024
N_HEADS = 8
HEAD_DIM = 64
D_ATTN = N_HEADS * HEAD_DIM
D_CONV = D_MODEL // 2
CONV_K = 31
CONV_HALO = 32
MOBA_BLOCK = 256
MOBA_TOPK = 3
PAGE_SIZE = 128
PAGES_PER_BLOCK = MOBA_BLOCK // PAGE_SIZE
ROPE_THETA = 10000.0
D_FF = 2816
EPS = 1e-6
D_IN = 3 * D_ATTN + 2 * D_CONV + 2 * D_MODEL
LANES = 128
SUBLANES = 8
HEADS_PER_TILE = LANES // HEAD_DIM
N_HEAD_PAIRS = N_HEADS // HEADS_PER_TILE
FFN_CHUNK = 256
KEY_CHUNK_PAGES = 16
VMEM_LIMIT = 56 * 1024 * 1024

F32 = jnp.float32
BF16 = jnp.bfloat16
NEG_INF = float("-inf")
NT_DIMS = (((1,), (1,)), ((), ()))


def _resident(shape):
    return pl.BlockSpec(shape, lambda *_: (0,) * len(shape), pipeline_mode=pl.Buffered(1))


def _sigmoid(x):
    return 0.5 * jnp.tanh(0.5 * x) + 0.5


def _silu(x):
    return x * _sigmoid(x)


def _rms_mod(x, g, sc, sh):
    y = x * lax.rsqrt(jnp.mean(x * x, axis=-1, keepdims=True) + EPS) * g
    return y * (1.0 + sc) + sh


def _rope(y, cos, sin):
    reps = D_ATTN // LANES
    cos = jnp.concatenate([cos] * reps, axis=-1)
    sin = jnp.concatenate([sin] * reps, axis=-1)
    half = HEAD_DIM // 2
    lane = lax.broadcasted_iota(jnp.int32, y.shape, 1)
    first = (lane % HEAD_DIM) < half
    partner = jnp.where(first, pltpu.roll(y, D_ATTN - half, 1), pltpu.roll(y, half, 1))
    return y * cos + partner * sin


def _layer_norm_silu(c, g, b):
    mu = jnp.mean(c, axis=-1, keepdims=True)
    d = c - mu
    y = d * lax.rsqrt(jnp.mean(d * d, axis=-1, keepdims=True) + EPS) * g + b
    return _silu(y)


def _ada_kernel(c_ref, w_ref, b_ref, o_ref):
    c = c_ref[...]
    o_ref[0] = jnp.dot(_silu(c).astype(BF16), w_ref[0].astype(BF16), preferred_element_type=F32) + b_ref[0]


def _ada(c_all, w_ada, b_ada):
    depth, _, n_out = w_ada.shape
    rows = c_all.shape[0]
    tn = 1024
    return pl.pallas_call(
        _ada_kernel,
        out_shape=jax.ShapeDtypeStruct((depth, rows, n_out), F32),
        grid=(depth, n_out // tn),
        in_specs=[pl.BlockSpec((rows, D_MODEL), lambda l, j: (0, 0)),
                  pl.BlockSpec((1, D_MODEL, tn), lambda l, j: (l, 0, j)),
                  pl.BlockSpec((1, 1, tn), lambda l, j: (l, 0, j))],
        out_specs=pl.BlockSpec((1, rows, tn), lambda l, j: (l, 0, j)),
        compiler_params=pltpu.CompilerParams(dimension_semantics=("arbitrary", "arbitrary"),
                                             vmem_limit_bytes=VMEM_LIMIT),
        name="ada_mod",
    )(c_all, w_ada, b_ada.reshape(depth, 1, n_out))


def _inproj_prompt_kernel(pt_ref, x_ref, sh_ref, sc_ref, g_ref, w_ref, cos_ref, sin_ref, cw_ref, cb_ref, lng_ref,
                          lnb_ref, ck_ref,
                          q_ref, k_ref, v_ref, kb_ref, vt_ref, km_ref, cbn_ref, ga_ref, gb_ref, cs_ref, kmt_ref,
                          ubuf, zbuf, kbuf, ksems, *, tm, nt, stream):
    t = pl.program_id(1)
    layer, n_pages, n_seq, part = stream
    n_chunks, do_chunk = _key_mean_streamer(pt_ref, ck_ref, kmt_ref, kbuf, ksems, seq=pl.program_id(0) * nt + t,
                                            layer=layer, n_pages=n_pages, n_seq=n_seq, part=part)
    key_chunks = iter(range(n_chunks))

    def stream_keys(share):
        for _ in range(round(n_chunks * share)):
            c = next(key_chunks, None)
            if c is not None:
                do_chunk(c)

    stream_keys(0.25)
    hb = _rms_mod(x_ref[0], g_ref[...], sc_ref[0, 0], sh_ref[0, 0]).astype(BF16)

    def proj(lo, hi):
        return jnp.dot(hb, w_ref[:, lo:hi], preferred_element_type=F32)

    cos, sin = cos_ref[...], sin_ref[...]
    q_ref[0] = _rope(proj(0, D_ATTN), cos, sin)
    k = _rope(proj(D_ATTN, 2 * D_ATTN), cos, sin)
    k_ref[0] = k
    kb_ref[0] = k.astype(BF16)
    for j in range(tm // MOBA_BLOCK):
        km_ref[0, j] = jnp.sum(k[j * MOBA_BLOCK:(j + 1) * MOBA_BLOCK], axis=0, keepdims=True) * (1.0 / MOBA_BLOCK)
    v = proj(2 * D_ATTN, 3 * D_ATTN)
    v_ref[0] = v
    for j in range(tm // MOBA_BLOCK):
        vt_ref[0, j] = v[j * MOBA_BLOCK:(j + 1) * MOBA_BLOCK].T.astype(BF16)
    stream_keys(0.25)
    u0 = 3 * D_ATTN
    glu = proj(u0, u0 + D_CONV) * _sigmoid(proj(u0 + D_CONV, u0 + 2 * D_CONV))
    g0 = u0 + 2 * D_CONV
    ga_ref[0] = _sigmoid(proj(g0, g0 + D_MODEL)).astype(BF16)
    gb_ref[0] = _sigmoid(proj(g0 + D_MODEL, g0 + 2 * D_MODEL)).astype(BF16)
    stream_keys(0.25)

    @pl.when(t == 0)
    def _():
        ubuf[0:CONV_HALO, :] = jnp.zeros((CONV_HALO, D_CONV), F32)

    ubuf[CONV_HALO:CONV_HALO + tm, :] = glu
    acc = jnp.zeros((tm, D_CONV), F32) + cb_ref[...]
    for r in range(SUBLANES):
        z = None
        for a in range(-(-CONV_K // SUBLANES)):
            d = SUBLANES * a + r
            if d < CONV_K:
                term = (cw_ref[pl.ds(CONV_K - 1 - d, 1), :]
                        * ubuf[pl.ds(CONV_HALO - SUBLANES * (a + 1), tm + SUBLANES), :])
                z = term if z is None else z + term
        zbuf[...] = z
        acc = acc + zbuf[pl.ds(SUBLANES - r, tm), :]
    cbn_ref[0] = _layer_norm_silu(acc, lng_ref[...], lnb_ref[...]).astype(BF16)
    stream_keys(1.0)

    @pl.when(t == nt - 1)
    def _():
        cs_ref[0] = ubuf[pl.ds(CONV_HALO + tm - (CONV_K - 1), CONV_K - 1), :]

    ubuf[0:CONV_HALO, :] = ubuf[tm:tm + CONV_HALO, :]


def _inproj_prompt(x, mod4, l, g_mix, w_in, cos, sin, conv_w, conv_b, ln_g, ln_b, key_stream, *, tm):
    bsz, s, _ = x.shape
    nt = s // tm
    nb = s // MOBA_BLOCK
    page_table, cache_kt, part = key_stream
    db, n_pages = page_table.shape
    assert bsz * nt == db, "one sample sequence's keys are streamed per grid step"
    key_spec, key_shape, key_scratch = _key_stream_specs(db, part)
    row = lambda b, t, *_: (b, t, 0)
    vec = lambda n: pl.BlockSpec((1, n), lambda b, t, *_: (0, 0))
    mod = lambda j: pl.BlockSpec((1, 1, 1, D_MODEL), lambda b, t, *_: (l, b, 0, j))
    tok = lambda n: pl.BlockSpec((1, tm, n), row)
    out_shape = (
        jax.ShapeDtypeStruct((bsz, s, D_ATTN), F32),
        jax.ShapeDtypeStruct((bsz, s, D_ATTN), F32),
        jax.ShapeDtypeStruct((bsz, s, D_ATTN), F32),
        jax.ShapeDtypeStruct((bsz, s, D_ATTN), BF16),
        jax.ShapeDtypeStruct((bsz, nb, D_ATTN, MOBA_BLOCK), BF16),
        jax.ShapeDtypeStruct((bsz, nb, 1, D_ATTN), F32),
        jax.ShapeDtypeStruct((bsz, s, D_CONV), BF16),
        jax.ShapeDtypeStruct((bsz, s, D_MODEL), BF16),
        jax.ShapeDtypeStruct((bsz, s, D_MODEL), BF16),
        jax.ShapeDtypeStruct((bsz, CONV_K - 1, D_CONV), F32),
        key_shape,
    )
    out_specs = (tok(D_ATTN), tok(D_ATTN), tok(D_ATTN), tok(D_ATTN),
                 pl.BlockSpec((1, tm // MOBA_BLOCK, D_ATTN, MOBA_BLOCK), lambda b, t, *_: (b, t, 0, 0)),
                 pl.BlockSpec((1, tm // MOBA_BLOCK, 1, D_ATTN), lambda b, t, *_: (b, t, 0, 0)),
                 tok(D_CONV), tok(D_MODEL), tok(D_MODEL),
                 pl.BlockSpec((1, CONV_K - 1, D_CONV), lambda b, t, *_: (b, 0, 0)),
                 key_spec(lambda b, t, *_: (b * nt + t, 0, 0)))
    grid_spec = pltpu.PrefetchScalarGridSpec(
        num_scalar_prefetch=1,
        grid=(bsz, nt),
        in_specs=[tok(D_MODEL), mod(0), mod(1), vec(D_MODEL), _resident((D_MODEL, D_IN)),
                  pl.BlockSpec((tm, LANES), lambda b, t, *_: (t, 0)), pl.BlockSpec((tm, LANES), lambda b, t, *_: (t, 0)),
                  pl.BlockSpec((CONV_K, D_CONV), lambda b, t, *_: (0, 0)), vec(D_CONV), vec(D_CONV), vec(D_CONV),
                  pl.BlockSpec(memory_space=pl.ANY)],
        out_specs=out_specs,
        scratch_shapes=[pltpu.VMEM((CONV_HALO + tm, D_CONV), F32), pltpu.VMEM((tm + SUBLANES, D_CONV), F32)]
        + key_scratch,
    )
    return pl.pallas_call(
        functools.partial(_inproj_prompt_kernel, tm=tm, nt=nt, stream=(l, n_pages, db, part)),
        out_shape=out_shape,
        grid_spec=grid_spec,
        compiler_params=pltpu.CompilerParams(dimension_semantics=("arbitrary", "arbitrary"),
                                             vmem_limit_bytes=VMEM_LIMIT),
        name="inproj_prompt",
    )(page_table.reshape(-1), x, mod4, mod4, g_mix, w_in, cos, sin, conv_w, conv_b, ln_g, ln_b, cache_kt)


def _head_rows_mask(rows):
    lane_head = lax.broadcasted_iota(jnp.int32, (rows, D_ATTN), 1) // HEAD_DIM
    return lane_head == lax.broadcasted_iota(jnp.int32, (rows, D_ATTN), 0) % N_HEADS


def _attn_prompt_kernel(q_ref, k_ref, vt_ref, km_ref, o_ref, qt_ref, sel_ref, *, nb):
    qb = pl.program_id(1)
    tq = MOBA_BLOCK
    q = q_ref[0]

    hm = _head_rows_mask(N_HEADS)
    km = km_ref[0]
    kmt = jnp.concatenate([jnp.where(hm, km[n:n + 1, :], 0.0) for n in range(nb)], axis=0)
    gate = lax.dot_general(kmt, q, NT_DIMS, precision=lax.Precision.HIGHEST, preferred_element_type=F32)
    g = [gate[n * N_HEADS:(n + 1) * N_HEADS, :] for n in range(nb)]
    for n in range(nb - 1):
        rank = jnp.zeros((N_HEADS, tq), F32)
        for m in range(nb - 1):
            if m != n:
                beats = (g[m] >= g[n]) if m < n else (g[m] > g[n])
                rank = rank + jnp.where(beats, (m < qb).astype(F32), 0.0)
        sel_ref[n] = jnp.where(rank < MOBA_TOPK, 1.0, 0.0)

    qt = (q * HEAD_DIM ** -0.5).T
    sub = lax.broadcasted_iota(jnp.int32, (LANES, tq), 0)
    for p in range(N_HEAD_PAIRS):
        pair = qt[p * LANES:(p + 1) * LANES, :]
        qt_ref[2 * p] = jnp.where(sub < HEAD_DIM, pair, 0.0).astype(BF16)
        qt_ref[2 * p + 1] = jnp.where(sub < HEAD_DIM, 0.0, pair).astype(BF16)

    def block(n, start, mask_of, state):
        ms, ls, accs = (list(s) for s in state) if state is not None else (None, None, None)
        scores = []
        for p in range(N_HEAD_PAIRS):
            kk = k_ref[0, pl.ds(start, MOBA_BLOCK), p * LANES:(p + 1) * LANES]
            for hh in range(HEADS_PER_TILE):
                scores.append(jnp.dot(kk, qt_ref[HEADS_PER_TILE * p + hh], preferred_element_type=F32))
        out_m, out_l, out_acc = [], [], []
        for h in range(N_HEADS):
            s = jnp.where(mask_of(h), scores[h], NEG_INF)
            m_blk = jnp.max(s, axis=0, keepdims=True)
            m_new = m_blk if state is None else jnp.maximum(ms[h], m_blk)
            e = jnp.exp(s - m_new)
            l_new = jnp.sum(e, axis=0, keepdims=True)
            pv = jnp.dot(vt_ref[0, n, h * HEAD_DIM:(h + 1) * HEAD_DIM, :], e.astype(BF16),
                         preferred_element_type=F32)
            if state is not None:
                alpha = jnp.exp(ms[h] - m_new)
                l_new = alpha * ls[h] + l_new
                pv = alpha * accs[h] + pv
            out_m.append(m_new)
            out_l.append(l_new)
            out_acc.append(pv)
        return tuple(out_m), tuple(out_l), tuple(out_acc)

    causal = (lax.broadcasted_iota(jnp.int32, (tq, tq), 0) <= lax.broadcasted_iota(jnp.int32, (tq, tq), 1))
    state = block(qb, pl.multiple_of(qb * MOBA_BLOCK, MOBA_BLOCK), lambda h: causal, None)

    def past_block(n, state):
        return block(n, pl.multiple_of(n * MOBA_BLOCK, MOBA_BLOCK), lambda h: sel_ref[n, h:h + 1, :] > 0.5, state)

    _, ls, accs = lax.fori_loop(0, qb, past_block, state)
    out = jnp.concatenate([accs[h] / ls[h] for h in range(N_HEADS)], axis=0)
    o_ref[0] = out.T.astype(BF16)


def _attn_prompt(q, kb, vt, kmean):
    bsz, s, _ = q.shape
    nb = s // MOBA_BLOCK
    return pl.pallas_call(
        functools.partial(_attn_prompt_kernel, nb=nb),
        out_shape=jax.ShapeDtypeStruct((bsz, s, D_ATTN), BF16),
        grid=(bsz, nb),
        in_specs=[pl.BlockSpec((1, MOBA_BLOCK, D_ATTN), lambda b, i: (b, i, 0)),
                  pl.BlockSpec((1, s, D_ATTN), lambda b, i: (b, 0, 0)),
                  pl.BlockSpec((1, nb, D_ATTN, MOBA_BLOCK), lambda b, i: (b, 0, 0, 0)),
                  pl.BlockSpec((1, nb, D_ATTN), lambda b, i: (b, 0, 0))],
        out_specs=pl.BlockSpec((1, MOBA_BLOCK, D_ATTN), lambda b, i: (b, i, 0)),
        scratch_shapes=[pltpu.VMEM((N_HEADS, LANES, MOBA_BLOCK), BF16),
                        pltpu.VMEM((max(nb - 1, 1), N_HEADS, MOBA_BLOCK), F32)],
        compiler_params=pltpu.CompilerParams(dimension_semantics=("arbitrary", "arbitrary"),
                                             vmem_limit_bytes=VMEM_LIMIT),
        name="attn_prompt",
    )(q, kb, vt, kmean)


N_MLP_INPUTS = 16


def _key_mean_streamer(pt_ref, ck_ref, kmt_ref, buf, sems, *, seq, layer, n_pages, n_seq, part):
    page0, n_part = part
    n_chunks = n_part // KEY_CHUNK_PAGES
    assert n_chunks % 2 == 0, "chunk parity must carry over from one grid step to the next"
    blocks_per_chunk = KEY_CHUNK_PAGES // PAGES_PER_BLOCK

    def copies(s, c, slot):
        return [pltpu.make_async_copy(ck_ref.at[layer, pt_ref[s * n_pages + page0 + c * KEY_CHUNK_PAGES + g]],
                                      buf.at[slot, g], sems.at[slot]) for g in range(KEY_CHUNK_PAGES)]

    @pl.when(seq == 0)
    def _():
        for cp in copies(0, 0, 0):
            cp.start()

    def do_chunk(c):
        slot = c % 2
        if c + 1 < n_chunks:
            for cp in copies(seq, c + 1, 1 - slot):
                cp.start()
        else:
            @pl.when(seq + 1 < n_seq)
            def _():
                for cp in copies(seq + 1, 0, 1 - slot):
                    cp.start()
        for cp in copies(seq, c, slot):
            cp.wait()
        for j in range(blocks_per_chunk):
            tot = buf[slot, j * PAGES_PER_BLOCK]
            for g in range(1, PAGES_PER_BLOCK):
                tot = tot + buf[slot, j * PAGES_PER_BLOCK + g]
            blk = c * blocks_per_chunk + j
            kmt_ref[0, :, blk:blk + 1] = (jnp.sum(tot.reshape(D_ATTN, PAGE_SIZE), axis=-1, keepdims=True)
                                          * (1.0 / MOBA_BLOCK))

    return n_chunks, do_chunk


def _key_stream_specs(db, part):
    n_blk = part[1] // PAGES_PER_BLOCK
    assert part[1] % (2 * KEY_CHUNK_PAGES) == 0 and KEY_CHUNK_PAGES % PAGES_PER_BLOCK == 0
    spec = lambda index_map: pl.BlockSpec((1, D_ATTN, n_blk), index_map)
    scratch = [pltpu.VMEM((2, KEY_CHUNK_PAGES, N_HEADS, HEAD_DIM, PAGE_SIZE), F32), pltpu.SemaphoreType.DMA((2,))]
    return spec, jax.ShapeDtypeStruct((db, D_ATTN, n_blk), F32), scratch


def _mlp_kernel(*refs, final, stream):
    if stream is None:
        ins, (o_ref,) = refs[:N_MLP_INPUTS], refs[N_MLP_INPUTS:]
        n_chunks, do_chunk = 0, None
    else:
        pt_ref, ins, ck_ref = refs[0], refs[1:1 + N_MLP_INPUTS], refs[1 + N_MLP_INPUTS]
        o_ref, kmt_ref, buf, sems = refs[2 + N_MLP_INPUTS:]
        layer, n_pages, n_seq, part = stream
        n_chunks, do_chunk = _key_mean_streamer(pt_ref, ck_ref, kmt_ref, buf, sems, seq=pl.program_id(0),
                                                layer=layer, n_pages=n_pages, n_seq=n_seq, part=part)
    (x_ref, a_ref, cbn_ref, ga_ref, gb_ref, gtm_ref, shf_ref, scf_ref, gtf_ref, gffn_ref, gfin_ref,
     wao_ref, wco_ref, wout_ref, wfi_ref, wfo_ref) = ins
    ffn_steps = list(range(0, D_FF, FFN_CHUNK))
    spread = max(n_chunks - 2, 0)
    at_step = {((i + 1) * len(ffn_steps)) // (spread + 1): 2 + i for i in range(spread)}
    assert len(at_step) == spread

    if n_chunks > 0:
        do_chunk(0)
    a = jnp.dot(a_ref[...].astype(BF16), wao_ref[...], preferred_element_type=F32)
    cb = jnp.dot(cbn_ref[...], wco_ref[...], preferred_element_type=F32)
    merged = ga_ref[...].astype(F32) * a + gb_ref[...].astype(F32) * cb
    x = x_ref[...] + gtm_ref[...] * jnp.dot(merged.astype(BF16), wout_ref[...], preferred_element_type=F32)
    hb = _rms_mod(x, gffn_ref[...], scf_ref[...], shf_ref[...]).astype(BF16)
    if n_chunks > 1:
        do_chunk(1)

    def gate_up(c):
        return (jnp.dot(hb, wfi_ref[:, c:c + FFN_CHUNK], preferred_element_type=F32),
                jnp.dot(hb, wfi_ref[:, D_FF + c:D_FF + c + FFN_CHUNK], preferred_element_type=F32))

    acc = jnp.zeros(x.shape, F32)
    nxt = gate_up(0)
    for i, c in enumerate(ffn_steps):
        gate, up = nxt
        if c + FFN_CHUNK < D_FF:
            nxt = gate_up(c + FFN_CHUNK)
        acc = acc + jnp.dot((_silu(gate) * up).astype(BF16), wfo_ref[c:c + FFN_CHUNK, :], preferred_element_type=F32)
        if i in at_step:
            do_chunk(at_step[i])
    x = x + gtf_ref[...] * acc
    if final:
        x = x * lax.rsqrt(jnp.mean(x * x, axis=-1, keepdims=True) + EPS) * gfin_ref[...]
    o_ref[...] = x


def _mlp(x, attn, cbn, ga, gb, mod_rows, g_ffn, g_final, w_attn_o, w_conv_o, w_out, w_ffn_in, w_ffn_o,
         *, tm, rows_per_mod, final, key_stream=None):
    m = x.shape[0]
    tok = lambda n: pl.BlockSpec((tm, n), lambda i, *_: (i, 0))
    vec = lambda n: pl.BlockSpec((1, n), lambda i, *_: (0, 0))
    if rows_per_mod == 1:
        mod = lambda j: pl.BlockSpec((tm, D_MODEL), lambda i, *_: (i, j))
    else:
        assert rows_per_mod % tm == 0
        mod_rows = mod_rows.reshape(mod_rows.shape[0], 1, 6 * D_MODEL)
        mod = lambda j: pl.BlockSpec((None, 1, D_MODEL), lambda i, *_: (i // (rows_per_mod // tm), 0, j))
    in_specs = [tok(D_MODEL), tok(D_ATTN), tok(D_CONV), tok(D_MODEL), tok(D_MODEL),
                mod(2), mod(3), mod(4), mod(5), vec(D_MODEL), vec(D_MODEL),
                _resident((D_ATTN, D_MODEL)), _resident((D_CONV, D_MODEL)), _resident((D_MODEL, D_MODEL)),
                _resident((D_MODEL, 2 * D_FF)), _resident((D_FF, D_MODEL))]
    assert len(in_specs) == N_MLP_INPUTS
    args = (x, attn, cbn, ga, gb, mod_rows, mod_rows, mod_rows, mod_rows, g_ffn, g_final,
            w_attn_o, w_conv_o, w_out, w_ffn_in, w_ffn_o)
    params = pltpu.CompilerParams(dimension_semantics=("arbitrary",), vmem_limit_bytes=VMEM_LIMIT)
    x_shape = jax.ShapeDtypeStruct((m, D_MODEL), F32)
    if key_stream is None:
        return pl.pallas_call(
            functools.partial(_mlp_kernel, final=final, stream=None),
            out_shape=x_shape, grid=(m // tm,), in_specs=in_specs, out_specs=tok(D_MODEL),
            compiler_params=params, name="mixer_out_ffn",
        )(*args)
    page_table, cache_kt, layer, part = key_stream
    db, n_pages = page_table.shape
    assert m // tm == db, "one sample sequence's keys are streamed per grid step"
    key_spec, key_shape, key_scratch = _key_stream_specs(db, part)
    grid_spec = pltpu.PrefetchScalarGridSpec(
        num_scalar_prefetch=1,
        grid=(m // tm,),
        in_specs=in_specs + [pl.BlockSpec(memory_space=pl.ANY)],
        out_specs=(tok(D_MODEL), key_spec(lambda i, *_: (i, 0, 0))),
        scratch_shapes=key_scratch,
    )
    return pl.pallas_call(
        functools.partial(_mlp_kernel, final=final, stream=(layer, n_pages, db, part)),
        out_shape=(x_shape, key_shape),
        grid_spec=grid_spec, compiler_params=params, name="mixer_out_ffn_keys",
    )(page_table.reshape(-1), *args, cache_kt)


def _inproj_sample_kernel(x_ref, sh_ref, sc_ref, g_ref, w_ref, cos_ref, sin_ref, st_ref, cw_ref, cb_ref, lng_ref,
                          lnb_ref, q_ref, k_ref, v_ref, cbn_ref, ga_ref, gb_ref, cs_ref, uc, *, db, t_new):
    hb = _rms_mod(x_ref[...], g_ref[...], sc_ref[...], sh_ref[...]).astype(BF16)

    def proj(lo, hi):
        return jnp.dot(hb, w_ref[:, lo:hi], preferred_element_type=F32)

    cos, sin = cos_ref[...], sin_ref[...]
    q_ref[...] = _rope(proj(0, D_ATTN), cos, sin)
    k_ref[...] = _rope(proj(D_ATTN, 2 * D_ATTN), cos, sin)
    v_ref[...] = proj(2 * D_ATTN, 3 * D_ATTN)
    u0 = 3 * D_ATTN
    glu = proj(u0, u0 + D_CONV) * _sigmoid(proj(u0 + D_CONV, u0 + 2 * D_CONV))
    g0 = u0 + 2 * D_CONV
    ga_ref[...] = _sigmoid(proj(g0, g0 + D_MODEL)).astype(BF16)
    gb_ref[...] = _sigmoid(proj(g0 + D_MODEL, g0 + 2 * D_MODEL)).astype(BF16)

    ctx = CONV_K - 1
    uc[:, 0:ctx, :] = st_ref[...]
    uc[:, ctx:ctx + t_new, :] = glu.reshape(db, t_new, D_CONV)
    acc = jnp.zeros((db, t_new, D_CONV), F32) + cb_ref[...]
    for j in range(CONV_K):
        acc = acc + cw_ref[pl.ds(j, 1), :] * uc[:, pl.ds(j, t_new), :]
    cbn_ref[...] = _layer_norm_silu(acc.reshape(db * t_new, D_CONV), lng_ref[...], lnb_ref[...]).astype(BF16)
    cs_ref[...] = uc[:, pl.ds(t_new, ctx), :]


def _inproj_sample(x, sh, sc, g_mix, w_in, cos, sin, state, conv_w, conv_b, ln_g, ln_b, *, db, t_new):
    m = db * t_new
    full = lambda *shape: pl.BlockSpec(shape, lambda i: (0,) * len(shape))
    out_shape = (
        jax.ShapeDtypeStruct((m, D_ATTN), F32), jax.ShapeDtypeStruct((m, D_ATTN), F32),
        jax.ShapeDtypeStruct((m, D_ATTN), F32), jax.ShapeDtypeStruct((m, D_CONV), BF16),
        jax.ShapeDtypeStruct((m, D_MODEL), BF16), jax.ShapeDtypeStruct((m, D_MODEL), BF16),
        jax.ShapeDtypeStruct((db, CONV_K - 1, D_CONV), F32),
    )
    ctx_rows = -(-(CONV_K - 1 + t_new) // SUBLANES) * SUBLANES
    return pl.pallas_call(
        functools.partial(_inproj_sample_kernel, db=db, t_new=t_new),
        out_shape=out_shape,
        grid=(1,),
        in_specs=[full(m, D_MODEL), pl.BlockSpec((m, D_MODEL), lambda i: (0, 0)),
                  pl.BlockSpec((m, D_MODEL), lambda i: (0, 1)), full(1, D_MODEL), full(D_MODEL, D_IN),
                  full(m, LANES), full(m, LANES), full(db, CONV_K - 1, D_CONV), full(CONV_K, D_CONV),
                  full(1, D_CONV), full(1, D_CONV), full(1, D_CONV)],
        out_specs=(full(m, D_ATTN), full(m, D_ATTN), full(m, D_ATTN), full(m, D_CONV), full(m, D_MODEL),
                   full(m, D_MODEL), full(db, CONV_K - 1, D_CONV)),
        scratch_shapes=[pltpu.VMEM((db, ctx_rows, D_CONV), F32)],
        compiler_params=pltpu.CompilerParams(dimension_semantics=("arbitrary",), vmem_limit_bytes=VMEM_LIMIT),
        name="inproj_sample",
    )(x, sh, sc, g_mix, w_in, cos, sin, state, conv_w, conv_b, ln_g, ln_b)


def _select_kernel(q_ref, kmt_ref, idx_ref, *, t_new):
    n_blk = kmt_ref.shape[-1]
    rows = N_HEADS * t_new
    q = q_ref[...]
    lane_head = lax.broadcasted_iota(jnp.int32, (t_new, D_ATTN), 1) // HEAD_DIM
    qbd = jnp.concatenate([jnp.where(lane_head == h, q, 0.0) for h in range(N_HEADS)], axis=0)
    gate = jnp.dot(qbd, kmt_ref[0], precision=lax.Precision.HIGHEST, preferred_element_type=F32)
    col = lax.broadcasted_iota(jnp.int32, (rows, n_blk), 1)
    out_lane = lax.broadcasted_iota(jnp.int32, (rows, LANES), 1)
    out = jnp.zeros((rows, LANES), jnp.int32)
    for r in range(MOBA_TOPK):
        best = jnp.max(gate, axis=-1, keepdims=True)
        idx = jnp.min(jnp.where(gate == best, col, n_blk), axis=-1, keepdims=True)
        out = jnp.where(out_lane == r, idx, out)
        gate = jnp.where(col == idx, NEG_INF, gate)
    idx_ref[0] = out


def _select_blocks(q, key_means, *, t_new):
    db, _, n_blk = key_means.shape
    rows = N_HEADS * t_new
    return pl.pallas_call(
        functools.partial(_select_kernel, t_new=t_new),
        out_shape=jax.ShapeDtypeStruct((db, rows, LANES), jnp.int32),
        grid=(db,),
        in_specs=[pl.BlockSpec((t_new, D_ATTN), lambda b: (b, 0)),
                  pl.BlockSpec((1, D_ATTN, n_blk), lambda b: (b, 0, 0))],
        out_specs=pl.BlockSpec((1, rows, LANES), lambda b: (b, 0, 0)),
        compiler_params=pltpu.CompilerParams(dimension_semantics=("arbitrary",), vmem_limit_bytes=VMEM_LIMIT),
        name="select_blocks",
    )(q, key_means)


def _attn_sample_kernel(sel_ref, pt_ref, q_ref, kn_ref, vn_ref, ck_ref, cv_ref, o_ref, kbuf, vbuf, sems,
                        *, layer, n_pages, t_new, db):
    b = pl.program_id(0)
    n_slots = t_new * MOBA_TOPK
    scale = HEAD_DIM ** -0.5

    def copies(seq, h, par, slot):
        blk = sel_ref[(seq * N_HEADS + h) * n_slots + slot]
        out = []
        for g in range(PAGES_PER_BLOCK):
            page = pt_ref[seq * n_pages + blk * PAGES_PER_BLOCK + g]
            keys = pl.ds(g * PAGE_SIZE, PAGE_SIZE)
            out.append(pltpu.make_async_copy(ck_ref.at[layer, page, h], kbuf.at[par, slot, :, keys], sems.at[0, par]))
            out.append(pltpu.make_async_copy(cv_ref.at[layer, page, h], vbuf.at[par, slot, :, keys], sems.at[1, par]))
        return out

    def start_head(seq, h, par):
        for slot in range(n_slots):
            for i, cp in enumerate(copies(seq, h, par, slot)):
                cp.start(priority=i % 2)

    def wait_head(par):
        pltpu.make_async_copy(kbuf.at[par], kbuf.at[par], sems.at[0, par]).wait()
        pltpu.make_async_copy(vbuf.at[par], vbuf.at[par], sems.at[1, par]).wait()

    @pl.when(b == 0)
    def _():
        start_head(0, 0, 0)

    row = lax.broadcasted_iota(jnp.int32, (t_new, MOBA_BLOCK), 0)
    tok = lax.broadcasted_iota(jnp.int32, (t_new, 1), 0)
    for h in range(N_HEADS):
        par = h % 2
        if h + 1 < N_HEADS:
            start_head(b, h + 1, 1 - par)
        else:
            @pl.when(b + 1 < db)
            def _():
                start_head(b + 1, 0, 1 - par)
        wait_head(par)

        head = slice(h * HEAD_DIM, (h + 1) * HEAD_DIM)
        q = q_ref[:, head] * scale
        qb16 = q.astype(BF16)

        s_all = [jnp.dot(qb16, kbuf[par, slot].astype(BF16), preferred_element_type=F32)
                 for slot in range(n_slots)]
        e = []
        for r in range(MOBA_TOPK):
            er = s_all[r]
            for t in range(1, t_new):
                er = jnp.where(row == t, s_all[t * MOBA_TOPK + r], er)
            e.append(er)
        kn, vn = kn_ref[:, head], vn_ref[:, head]
        s_new = [jnp.where(tok >= j, jnp.sum(q * kn[j:j + 1, :], axis=-1, keepdims=True), NEG_INF)
                 for j in range(t_new)]
        m = functools.reduce(jnp.maximum, s_new + [jnp.max(er, axis=-1, keepdims=True) for er in e])
        p = [jnp.exp(er - m) for er in e]
        p_new = [jnp.exp(s - m) for s in s_new]
        den = functools.reduce(jnp.add, p_new + [jnp.sum(pr, axis=-1, keepdims=True) for pr in p])
        num = functools.reduce(jnp.add, [p_new[j] * vn[j:j + 1, :] for j in range(t_new)])

        parts = [lax.dot_general(jnp.where(row == t, p[r], 0.0).astype(BF16),
                                 vbuf[par, t * MOBA_TOPK + r].astype(BF16), NT_DIMS, preferred_element_type=F32)
                 for t in range(t_new) for r in range(MOBA_TOPK)]
        o_ref[:, head] = (num + functools.reduce(jnp.add, parts)) / den


def _attn_sample(sel, page_table, q, k_new, v_new, cache_kt, cache_vt, *, layer, t_new):
    db, n_pages = page_table.shape
    n_slots = t_new * MOBA_TOPK
    tok = pl.BlockSpec((t_new, D_ATTN), lambda b, s, pt: (b, 0))
    grid_spec = pltpu.PrefetchScalarGridSpec(
        num_scalar_prefetch=2,
        grid=(db,),
        in_specs=[tok, tok, tok, pl.BlockSpec(memory_space=pl.ANY), pl.BlockSpec(memory_space=pl.ANY)],
        out_specs=tok,
        scratch_shapes=[pltpu.VMEM((2, n_slots, HEAD_DIM, MOBA_BLOCK), F32),
                        pltpu.VMEM((2, n_slots, HEAD_DIM, MOBA_BLOCK), F32),
                        pltpu.SemaphoreType.DMA((2, 2))],
    )
    return pl.pallas_call(
        functools.partial(_attn_sample_kernel, layer=layer, n_pages=n_pages, t_new=t_new, db=db),
        out_shape=jax.ShapeDtypeStruct((db * t_new, D_ATTN), F32),
        grid_spec=grid_spec,
        compiler_params=pltpu.CompilerParams(dimension_semantics=("arbitrary",), vmem_limit_bytes=VMEM_LIMIT),
        name="attn_sample",
    )(sel.reshape(-1), page_table.reshape(-1), q, k_new, v_new, cache_kt, cache_vt)


def _rope_tables(pos):
    half = HEAD_DIM // 2
    inv_freq = ROPE_THETA ** (-jnp.arange(half, dtype=F32) / half)
    ang = pos.astype(F32)[:, None] * inv_freq[None, :]
    cos, sin = jnp.cos(ang), jnp.sin(ang)
    cos = jnp.concatenate([cos, cos] * HEADS_PER_TILE, axis=-1)
    sin = jnp.concatenate([-sin, sin] * HEADS_PER_TILE, axis=-1)
    return cos, sin


def kernel(x_prompt, x_sample, cache_k, cache_v, state_conv, page_table, c_prompt, c_sample, w_ada, b_ada, g_mix, w_in, w_attn_o, conv_w, conv_b, conv_norm_g, conv_norm_b, w_conv_o, w_out, g_ffn, w_ffn_in, w_ffn_o, g_final):
    bsz, seq, _ = x_prompt.shape
    db, t_new, _ = x_sample.shape
    depth = w_in.shape[0]
    n_pages = page_table.shape[1]
    past = n_pages * PAGE_SIZE
    assert seq % MOBA_BLOCK == 0 and past % MOBA_BLOCK == 0 and t_new <= MOBA_BLOCK and t_new % SUBLANES == 0
    assert past // MOBA_BLOCK >= MOBA_TOPK
    assert cache_k.shape[2:] == (PAGE_SIZE, N_HEADS, HEAD_DIM)
    tm = 512 if seq % 512 == 0 else MOBA_BLOCK

    cos_p, sin_p = _rope_tables(jnp.arange(seq, dtype=jnp.int32))
    cos_s, sin_s = _rope_tables(past + jnp.arange(t_new, dtype=jnp.int32))
    cos_s, sin_s = jnp.tile(cos_s, (db, 1)), jnp.tile(sin_s, (db, 1))

    mod = _ada(jnp.concatenate([c_prompt, c_sample], axis=0), w_ada, b_ada)
    mod4 = mod.reshape(depth, bsz + db, 1, 6 * D_MODEL)
    w_in_b, w_attn_o_b, w_conv_o_b = w_in.astype(BF16), w_attn_o.astype(BF16), w_conv_o.astype(BF16)
    w_out_b, w_ffn_in_b, w_ffn_o_b = w_out.astype(BF16), w_ffn_in.astype(BF16), w_ffn_o.astype(BF16)
    row2 = lambda a, l: a[l][None, :]
    g_fin = g_final[None, :]
    by_token = lambda a: a.reshape(db, t_new, N_HEADS, HEAD_DIM)
    cache_kt = cache_k.transpose(0, 1, 3, 4, 2)
    cache_vt = cache_v.transpose(0, 1, 3, 4, 2)

    xp = x_prompt
    xs = x_sample.reshape(db * t_new, D_MODEL)
    kp, vp, cp, ksm, vsm, csm = [], [], [], [], [], []
    for l in range(depth):
        final = l == depth - 1
        lw = (row2(g_mix, l), w_in_b[l])
        cw = (conv_w[l], row2(conv_b, l), row2(conv_norm_g, l), row2(conv_norm_b, l))
        mw = (row2(g_ffn, l), g_fin, w_attn_o_b[l], w_conv_o_b[l], w_out_b[l], w_ffn_in_b[l], w_ffn_o_b[l])

        half = n_pages // 2
        q, k, v, kb, vt, kmean, cbn, ga, gb, cs, keys_lo = _inproj_prompt(
            xp, mod4, l, *lw, cos_p, sin_p, *cw, (page_table, cache_kt, (0, half)), tm=tm)
        attn = _attn_prompt(q, kb, vt, kmean.reshape(bsz, seq // MOBA_BLOCK, D_ATTN))
        flat = lambda a: a.reshape(bsz * seq, a.shape[-1])
        xp, keys_hi = _mlp(flat(xp), flat(attn), flat(cbn), flat(ga), flat(gb), mod[l, :bsz], *mw,
                           tm=tm, rows_per_mod=seq, final=final, key_stream=(page_table, cache_kt, l, (half, half)))
        key_means = jnp.concatenate([keys_lo, keys_hi], axis=-1)
        xp = xp.reshape(bsz, seq, D_MODEL)
        kp.append(k.reshape(bsz, seq, N_HEADS, HEAD_DIM))
        vp.append(v.reshape(bsz, seq, N_HEADS, HEAD_DIM))
        cp.append(cs)

        mod_s = jnp.repeat(mod[l, bsz:], t_new, axis=0)
        qs, ks, vs, cbn_s, ga_s, gb_s, cs_s = _inproj_sample(xs, mod_s, mod_s, *lw, cos_s, sin_s, state_conv[l], *cw,
                                                             db=db, t_new=t_new)
        sel = _select_blocks(qs, key_means, t_new=t_new)[:, :, :MOBA_TOPK]
        attn_s = _attn_sample(sel, page_table, qs, ks, vs, cache_kt, cache_vt, layer=l, t_new=t_new)
        xs = _mlp(xs, attn_s, cbn_s, ga_s, gb_s, mod_s, *mw, tm=db * t_new, rows_per_mod=1, final=final)
        ksm.append(by_token(ks))
        vsm.append(by_token(vs))
        csm.append(cs_s)

    return (xp, xs.reshape(db, t_new, D_MODEL), jnp.stack(kp), jnp.stack(vp), jnp.stack(cp),
            jnp.stack(ksm), jnp.stack(vsm), jnp.stack(csm))
```
